```python
import math
import jax, jax.numpy as jnp
from jax import lax
import numpy as np

D_MODEL = 4096
BATCH = 4
SEQ = 4096
DEPTH = 2

MEM_LEN = 256
GLA_HEADS = 8
GLA_DK = 64
GLA_DV = 128
GLA_RANK = 16
GLA_TAU = 16.0
GLA_CHUNK = 64
MOBA_HEADS = 16
MOBA_DH = 128
MOBA_BLOCK = 256
MOBA_TOPK = 3
MOBA_QBLOCK = 16
GDN_HEADS = 8
GDN_DK = 128
GDN_DV = 128
GDN_CONV = 4
GDN_CHUNK = 64
CROSS_HEADS = 4
CROSS_DH = 128
MLP_HIDDEN = 4 * D_MODEL
ROPE_THETA = 500000.0
ROT_DIM = MOBA_DH // 4
DN_ALPHA = (2.0 * DEPTH) ** 0.25
DN_BETA = (8.0 * DEPTH) ** -0.25
LN_EPS = 1e-5
RMS_EPS = 1e-6

GLA_QK_W = GLA_HEADS * GLA_DK
GLA_V_W = GLA_HEADS * GLA_DV
MOBA_W = MOBA_HEADS * MOBA_DH
GDN_K_W = GDN_HEADS * GDN_DK
GDN_V_W = GDN_HEADS * GDN_DV
GDN_CONV_W = 2 * GDN_K_W + GDN_V_W
MIX_WIDTH = GLA_V_W + MOBA_W + GDN_V_W
CROSS_W = CROSS_HEADS * CROSS_DH
IN_SPLITS = [GLA_QK_W, GLA_QK_W, GLA_V_W, GLA_RANK, GLA_V_W,
             MOBA_W, MOBA_W, MOBA_W,
             GDN_K_W, GDN_K_W, GDN_V_W, GDN_HEADS, GDN_HEADS, GDN_V_W]
N_IN = sum(IN_SPLITS)
IN_OFFSETS = [sum(IN_SPLITS[:i + 1]) for i in range(len(IN_SPLITS) - 1)]

kernel_name = "hybrid_gla_moba_gdn_deepnorm"


def split_heads(t, n):
    b, s, w = t.shape
    return t.reshape(b, s, n, w // n).transpose(0, 2, 1, 3)


def merge_heads(t):
    b, n, s, d = t.shape
    return t.transpose(0, 2, 1, 3).reshape(b, s, n * d)


def layer_norm(x, g, b):
    xf = x.astype(jnp.float32)
    mu = jnp.mean(xf, -1, keepdims=True)
    var = jnp.mean(jnp.square(xf - mu), -1, keepdims=True)
    return ((xf - mu) * lax.rsqrt(var + LN_EPS) * g + b).astype(x.dtype)


def gated_rms_norm(o, z, w):
    of = o.astype(jnp.float32)
    of = of * lax.rsqrt(jnp.mean(of * of, -1, keepdims=True) + RMS_EPS) * w
    return of * jax.nn.silu(z.astype(jnp.float32))


def l2_normalize(t):
    return t * lax.rsqrt(jnp.sum(t * t, -1, keepdims=True) + RMS_EPS)


def partial_rotary(t, positions):
    half = ROT_DIM // 2
    inv = jnp.power(ROPE_THETA, -jnp.arange(half, dtype=jnp.float32) * 2.0 / ROT_DIM)
    ang = positions.astype(jnp.float32)[:, None] * inv[None, :]
    cos, sin = jnp.cos(ang), jnp.sin(ang)
    tf = t.astype(jnp.float32)
    x1, x2, rest = tf[..., :half], tf[..., half:ROT_DIM], tf[..., ROT_DIM:]
    out = jnp.concatenate([x1 * cos - x2 * sin, x2 * cos + x1 * sin, rest], -1)
    return out.astype(t.dtype)


def gla_mixer(q, k, v, g_low, r, w_g2, b_g, norm_w):
    f32 = jnp.float32
    B, S, _ = q.shape
    C = GLA_CHUNK
    N = S // C
    log_a = jax.nn.log_sigmoid((g_low @ w_g2 + b_g).astype(f32)) / GLA_TAU

    def chunks(t, d):
        return t.astype(f32).reshape(B, N, C, GLA_HEADS, d).transpose(0, 3, 1, 2, 4)

    qc = chunks(q, GLA_DK) * GLA_DK ** -0.5
    kc = chunks(k, GLA_DK)
    vc = chunks(v, GLA_DV)
    G = jnp.cumsum(chunks(log_a, GLA_DK), axis=3)
    G_ref = G[:, :, :, C // 2 - 1:C // 2]
    causal = jnp.tril(jnp.ones((C, C), dtype=bool))
    A = jnp.einsum('bhncd,bhnsd->bhncs', qc * jnp.exp(G - G_ref), kc * jnp.exp(G_ref - G))
    A = jnp.where(causal, A, 0.0)
    o_intra = jnp.einsum('bhncs,bhnsv->bhncv', A, vc)
    G_last = G[:, :, :, -1]
    k_end = kc * jnp.exp(G_last[:, :, :, None, :] - G)
    U = jnp.einsum('bhncd,bhncv->bhndv', k_end, vc)

    def step(state, inp):
        dec, u = inp
        return dec[..., None] * state + u, state

    s0 = jnp.zeros((B, GLA_HEADS, GLA_DK, GLA_DV), f32)
    _, S_in = lax.scan(step, s0, (jnp.moveaxis(jnp.exp(G_last), 2, 0), jnp.moveaxis(U, 2, 0)))
    S_in = jnp.moveaxis(S_in, 0, 2)
    o_inter = jnp.einsum('bhncd,bhndv->bhncv', qc * jnp.exp(G), S_in)
    o = (o_intra + o_inter).transpose(0, 2, 3, 1, 4).reshape(B, S, GLA_HEADS, GLA_DV)
    o = gated_rms_norm(o, r.reshape(B, S, GLA_HEADS, GLA_DV), norm_w)
    return o.reshape(B, S, GLA_V_W).astype(q.dtype)


def moba_mixer(q, k, v, positions):
    f32 = jnp.float32
    B, S, _ = q.shape
    H, Dh, Bk, QB = MOBA_HEADS, MOBA_DH, MOBA_BLOCK, MOBA_QBLOCK
    q = partial_rotary(split_heads(q, H), positions)
    k = partial_rotary(split_heads(k, H), positions)
    v = split_heads(v, H)
    nb = -(-S // Bk)
    S_pad = nb * Bk
    padw = ((0, 0), (0, 0), (0, S_pad - S), (0, 0))
    q, k, v = jnp.pad(q, padw), jnp.pad(k, padw), jnp.pad(v, padw)
    k_blk = k.reshape(B, H, nb, Bk, Dh)
    v_blk = v.reshape(B, H, nb, Bk, Dh)
    k_mean = jnp.mean(k_blk.astype(f32), axis=3)
    q_blockid = jnp.arange(S_pad) // Bk
    topk = max(1, min(MOBA_TOPK, nb - 1))
    gate = jnp.einsum('bhsd,bhnd->bhsn', q.astype(f32), k_mean)
    gate = jnp.where(jnp.arange(nb)[None, :] < q_blockid[:, None], gate, -jnp.inf)
    _, sel = lax.top_k(gate, topk)
    nq = S_pad // QB
    q_c = q.reshape(B, H, nq, QB, Dh).transpose(2, 0, 1, 3, 4)
    sel_c = sel.reshape(B, H, nq, QB, topk).transpose(2, 0, 1, 3, 4)
    bi = jnp.arange(B)[:, None, None, None]
    hi = jnp.arange(H)[None, :, None, None]
    scale = Dh ** -0.5

    def attend(args):
        qq, ss, c = args
        qpos = c * QB + jnp.arange(QB)
        own = (c * QB) // Bk
        k_sel = k_blk[bi, hi, ss]
        v_sel = v_blk[bi, hi, ss]
        valid = jnp.arange(topk)[None, :] < (qpos // Bk)[:, None]
        s_sel = jnp.einsum('bhqd,bhqknd->bhqkn', qq, k_sel).astype(f32) * scale
        s_sel = jnp.where(valid[None, None, :, :, None], s_sel, -jnp.inf)
        k_own = lax.dynamic_index_in_dim(k_blk, own, axis=2, keepdims=False)
        v_own = lax.dynamic_index_in_dim(v_blk, own, axis=2, keepdims=False)
        kpos = own * Bk + jnp.arange(Bk)
        s_own = jnp.einsum('bhqd,bhnd->bhqn', qq, k_own).astype(f32) * scale
        s_own = jnp.where(kpos[None, :] <= qpos[:, None], s_own, -jnp.inf)
        s = jnp.concatenate([s_sel.reshape(B, H, QB, topk * Bk), s_own], -1)
        p = jax.nn.softmax(s, axis=-1).astype(v_blk.dtype)
        p_sel = p[..., :topk * Bk].reshape(B, H, QB, topk, Bk)
        p_own = p[..., topk * Bk:]
        return (jnp.einsum('bhqkn,bhqknd->bhqd', p_sel, v_sel)
                + jnp.einsum('bhqn,bhnd->bhqd', p_own, v_own))

    o = lax.map(attend, (q_c, sel_c, jnp.arange(nq)))
    o = o.transpose(1, 2, 0, 3, 4).reshape(B, H, S_pad, Dh)[:, :, :S]
    return merge_heads(o)


def gdn_mixer(q, k, v, a, b, z, conv_w, a_log, dt_bias, norm_w):
    f32 = jnp.float32
    B, S, _ = q.shape
    qkv = jnp.concatenate([q, k, v], -1)
    qkv = lax.conv_general_dilated(
        qkv, conv_w[:, None, :], window_strides=(1,), padding=[(GDN_CONV - 1, 0)],
        dimension_numbers=('NWC', 'WIO', 'NWC'), feature_group_count=GDN_CONV_W)
    qkv = jax.nn.silu(qkv)
    q, k, v = jnp.split(qkv, [GDN_K_W, 2 * GDN_K_W], axis=-1)
    C = GDN_CHUNK
    N = S // C

    def chunks(t):
        return t.astype(f32).reshape(B, N, C, GDN_HEADS, -1).transpose(0, 3, 1, 2, 4)

    def chunk_scalar(t):
        return t.reshape(B, N, C, GDN_HEADS).transpose(0, 3, 1, 2)

    qc = l2_normalize(chunks(q)) * GDN_DK ** -0.5
    kc = l2_normalize(chunks(k))
    vc = chunks(v)
    beta = chunk_scalar(jax.nn.sigmoid(b.astype(f32)))
    g = -jnp.exp(a_log.astype(f32)) * jax.nn.softplus(a.astype(f32) + dt_bias)
    gam = jnp.cumsum(chunk_scalar(g), axis=-1)
    incl = jnp.tril(jnp.ones((C, C), dtype=bool))
    strict = jnp.tril(jnp.ones((C, C), dtype=bool), k=-1)
    diff = gam[..., :, None] - gam[..., None, :]
    decay = jnp.where(incl, jnp.exp(jnp.where(incl, diff, 0.0)), 0.0)
    kk = jnp.einsum('bhncd,bhnsd->bhncs', kc, kc)
    A = jnp.where(strict, beta[..., :, None] * kk * decay, 0.0)
    T = jnp.eye(C, dtype=f32) + A
    rhs = jnp.concatenate([vc * beta[..., None], kc * (beta * jnp.exp(gam))[..., None]], -1)
    sol = lax.linalg.triangular_solve(T, rhs, left_side=True, lower=True, unit_diagonal=True)
    u, w = sol[..., :GDN_DV], sol[..., GDN_DV:]
    qk = jnp.where(incl, jnp.einsum('bhncd,bhnsd->bhncs', qc, kc) * decay, 0.0)
    q_dec = qc * jnp.exp(gam)[..., None]
    k_end = kc * jnp.exp(gam[..., -1:] - gam)[..., None]
    g_last = jnp.exp(gam[..., -1])

    def step(state, inp):
        u_n, w_n, q_n, qk_n, k_n, gl_n = inp
        v_new = u_n - jnp.einsum('bhcd,bhdv->bhcv', w_n, state)
        o_n = (jnp.einsum('bhcd,bhdv->bhcv', q_n, state)
               + jnp.einsum('bhcs,bhsv->bhcv', qk_n, v_new))
        state = gl_n[..., None, None] * state + jnp.einsum('bhcd,bhcv->bhdv', k_n, v_new)
        return state, o_n

    s0 = jnp.zeros((B, GDN_HEADS, GDN_DK, GDN_DV), f32)
    xs = (jnp.moveaxis(u, 2, 0), jnp.moveaxis(w, 2, 0), jnp.moveaxis(q_dec, 2, 0),
          jnp.moveaxis(qk, 2, 0), jnp.moveaxis(k_end, 2, 0), jnp.moveaxis(g_last, 2, 0))
    _, o = lax.scan(step, s0, xs)
    o = o.transpose(1, 0, 3, 2, 4).reshape(B, S, GDN_HEADS, GDN_DV)
    o = gated_rms_norm(o, z.reshape(B, S, GDN_HEADS, GDN_DV), norm_w)
    return o.reshape(B, S, GDN_V_W).astype(z.dtype)


def hybrid_mixer(h, positions, w_in, gla_w_g2, gla_b_g, gla_norm_w, gdn_conv_w, gdn_a_log,
                 gdn_dt_bias, gdn_norm_w, w_branch, w_gate, b_gate, w_out):
    proj = h @ w_in
    (gq, gk, gv, glow, gr, mq, mk, mv, dq, dk, dv, da, db, dz) = jnp.split(proj, IN_OFFSETS, axis=-1)
    y_gla = gla_mixer(gq, gk, gv, glow, gr, gla_w_g2, gla_b_g, gla_norm_w)
    y_moba = moba_mixer(mq, mk, mv, positions)
    y_gdn = gdn_mixer(dq, dk, dv, da, db, dz, gdn_conv_w, gdn_a_log, gdn_dt_bias, gdn_norm_w)
    wb_gla, wb_moba, wb_gdn = jnp.split(w_branch, [GLA_V_W, GLA_V_W + MOBA_W], axis=0)
    gates = jax.nn.sigmoid((h @ w_gate + b_gate).astype(jnp.float32)).astype(h.dtype)
    g_gla, g_moba, g_gdn = jnp.split(gates, 3, axis=-1)
    merged = g_gla * (y_gla @ wb_gla) + g_moba * (y_moba @ wb_moba) + g_gdn * (y_gdn @ wb_gdn)
    return merged @ w_out


def cross_attention(h, mem, w_q, w_k, w_v, w_o):
    q = split_heads(h @ w_q, CROSS_HEADS)
    k = split_heads(mem @ w_k, CROSS_HEADS)
    v = split_heads(mem @ w_v, CROSS_HEADS)
    s = jnp.einsum('bhsd,bhmd->bhsm', q, k).astype(jnp.float32) * CROSS_DH ** -0.5
    p = jax.nn.softmax(s, axis=-1).astype(v.dtype)
    return merge_heads(jnp.einsum('bhsm,bhmd->bhsd', p, v)) @ w_o


def squared_relu_mlp(h, w_up, w_down):
    a = jax.nn.relu(h @ w_up)
    return (a * a) @ w_down


def setup_inputs(seed: int = 0) -> dict:
    key = jax.random.key(seed)
    ks = iter(jax.random.split(key, 32))
    f32 = jnp.float32
    L, D = DEPTH, D_MODEL

    def nrm(shape, scale):
        return jax.random.normal(next(ks), shape, f32) * scale

    x = nrm((BATCH, SEQ, D), 1.0)
    mem = nrm((BATCH, MEM_LEN, D), 1.0)
    w_in = nrm((L, D, N_IN), D ** -0.5)
    gla_w_g2 = nrm((L, GLA_RANK, GLA_QK_W), GLA_RANK ** -0.5)
    gla_b_g = nrm((L, GLA_QK_W), 0.1)
    gla_norm_w = 1.0 + nrm((L, GLA_DV), 0.02)
    gdn_conv_w = nrm((L, GDN_CONV, GDN_CONV_W), GDN_CONV ** -0.5)
    gdn_a_log = jnp.log(jax.random.uniform(next(ks), (L, GDN_HEADS), f32, 1.0, 16.0))
    dt = jnp.exp(jax.random.uniform(next(ks), (L, GDN_HEADS), f32, math.log(1e-3), math.log(1e-1)))
    gdn_dt_bias = dt + jnp.log(-jnp.expm1(-dt))
    gdn_norm_w = 1.0 + nrm((L, GDN_DV), 0.02)
    row_scale = jnp.concatenate([jnp.full((GLA_V_W,), GLA_V_W ** -0.5, f32),
                                 jnp.full((MOBA_W,), MOBA_W ** -0.5, f32),
                                 jnp.full((GDN_V_W,), GDN_V_W ** -0.5, f32)])
    w_branch = nrm((L, MIX_WIDTH, D), 1.0) * row_scale[None, :, None]
    w_gate = nrm((L, D, 3 * D), D ** -0.5)
    b_gate = nrm((L, 3 * D), 0.02)
    w_out = nrm((L, D, D), D ** -0.5 * DN_BETA)
    ln1_g = 1.0 + nrm((L, D), 0.02)
    ln1_b = nrm((L, D), 0.02)
    w_cq = nrm((L, D, CROSS_W), D ** -0.5)
    w_ck = nrm((L, D, CROSS_W), D ** -0.5)
    w_cv = nrm((L, D, CROSS_W), D ** -0.5)
    w_co = nrm((L, CROSS_W, D), CROSS_W ** -0.5 * DN_BETA)
    ln2_g = 1.0 + nrm((L, D), 0.02)
    ln2_b = nrm((L, D), 0.02)
    w_up = nrm((L, D, MLP_HIDDEN), D ** -0.5)
    w_down = nrm((L, MLP_HIDDEN, D), MLP_HIDDEN ** -0.5 * DN_BETA)
    ln3_g = 1.0 + nrm((L, D), 0.02)
    ln3_b = nrm((L, D), 0.02)
    return {"x": x, "mem": mem, "w_in": w_in, "gla_w_g2": gla_w_g2, "gla_b_g": gla_b_g,
            "gla_norm_w": gla_norm_w, "gdn_conv_w": gdn_conv_w, "gdn_a_log": gdn_a_log,
            "gdn_dt_bias": gdn_dt_bias, "gdn_norm_w": gdn_norm_w, "w_branch": w_branch,
            "w_gate": w_gate, "b_gate": b_gate, "w_out": w_out, "ln1_g": ln1_g, "ln1_b": ln1_b,
            "w_cq": w_cq, "w_ck": w_ck, "w_cv": w_cv, "w_co": w_co, "ln2_g": ln2_g, "ln2_b": ln2_b,
            "w_up": w_up, "w_down": w_down, "ln3_g": ln3_g, "ln3_b": ln3_b}


def reference(x, mem, w_in, gla_w_g2, gla_b_g, gla_norm_w, gdn_conv_w, gdn_a_log, gdn_dt_bias,
              gdn_norm_w, w_branch, w_gate, b_gate, w_out, ln1_g, ln1_b, w_cq, w_ck, w_cv, w_co,
              ln2_g, ln2_b, w_up, w_down, ln3_g, ln3_b):
    positions = jnp.arange(x.shape[1], dtype=jnp.int32)
    h = x
    for l in range(DEPTH):
        mix = hybrid_mixer(h, positions, w_in[l], gla_w_g2[l], gla_b_g[l], gla_norm_w[l],
                           gdn_conv_w[l], gdn_a_log[l], gdn_dt_bias[l], gdn_norm_w[l],
                           w_branch[l], w_gate[l], b_gate[l], w_out[l])
        h = layer_norm(DN_ALPHA * h + mix, ln1_g[l], ln1_b[l])
        h = layer_norm(DN_ALPHA * h + cross_attention(h, mem, w_cq[l], w_ck[l], w_cv[l], w_co[l]),
                       ln2_g[l], ln2_b[l])
        h = layer_norm(DN_ALPHA * h + squared_relu_mlp(h, w_up[l], w_down[l]), ln3_g[l], ln3_b[l])
    return h
```

```python
import functools

import jax
import jax.numpy as jnp
from jax import lax
from jax.experimental import pallas as pl
from jax.experimental.pallas import tpu as pltpu

F32 = jnp.float32
BF16 = jnp.bfloat16
HIGHEST = lax.Precision.HIGHEST

GLA_HEADS, GLA_DK, GLA_DV, GLA_RANK, GLA_TAU, GLA_CHUNK = 8, 64, 128, 16, 16.0, 64
MOBA_HEADS, MOBA_DH, MOBA_BLOCK, MOBA_TOPK = 16, 128, 256, 3
GDN_HEADS, GDN_DK, GDN_DV, GDN_CONV, GDN_CHUNK = 8, 128, 128, 4, 64
CROSS_HEADS, CROSS_DH = 4, 128
ROPE_THETA = 500000.0
ROT_DIM = MOBA_DH // 4
LN_EPS = 1e-5
RMS_EPS = 1e-6

GLA_QK_W = GLA_HEADS * GLA_DK
GLA_V_W = GLA_HEADS * GLA_DV
MOBA_W = MOBA_HEADS * MOBA_DH
GDN_K_W = GDN_HEADS * GDN_DK
GDN_V_W = GDN_HEADS * GDN_DV

OFF_MQ = 0
OFF_MK = OFF_MQ + MOBA_W
OFF_MV = OFF_MK + MOBA_W
OFF_GQ = OFF_MV + MOBA_W
OFF_GK = OFF_GQ + GLA_QK_W
OFF_GV = OFF_GK + GLA_QK_W
OFF_GR = OFF_GV + GLA_V_W
OFF_DQ = OFF_GR + GLA_V_W
OFF_DK = OFF_DQ + GDN_K_W
OFF_DV = OFF_DK + GDN_K_W
OFF_DZ = OFF_DV + GDN_V_W
MAIN_W = OFF_DZ + GDN_V_W
SMALL_W = 128
SM_GLOW = 0
SM_DA = GLA_RANK
SM_DB = SM_DA + GDN_HEADS

LANE = 128
VMEM_LIMIT = 56 * 1024 * 1024


def _cparams(sem):
    return pltpu.CompilerParams(dimension_semantics=sem, vmem_limit_bytes=VMEM_LIMIT)


def _dot(a, b, **kw):
    return jnp.dot(a, b, preferred_element_type=F32, **kw)


def _dot_nt(a, b, **kw):
    return lax.dot_general(a, b, (((1,), (1,)), ((), ())), preferred_element_type=F32, **kw)


def _dot_tn(a, b, **kw):
    return lax.dot_general(a, b, (((0,), (0,)), ((), ())), preferred_element_type=F32, **kw)


def _sigmoid(x):
    return 1.0 / (1.0 + jnp.exp(-x))


def _softplus(x):
    return jnp.maximum(x, 0.0) + jnp.log1p(jnp.exp(-jnp.abs(x)))


def _log_sigmoid(x):
    return jnp.minimum(x, 0.0) - jnp.log1p(jnp.exp(-jnp.abs(x)))


def _mm_kernel(*refs, nk, epi, alpha):
    if epi in ("bias_sigmoid", "residual"):
        a_ref, w_ref, x_ref, o_ref = refs[:4]
        scratch = refs[4:]
    else:
        a_ref, w_ref, o_ref = refs[:3]
        x_ref = None
        scratch = refs[3:]

    def finish(acc):
        if epi == "bias_sigmoid":
            acc = _sigmoid(acc + x_ref[...])
        elif epi == "relu2":
            r = jnp.maximum(acc, 0.0)
            acc = r * r
        elif epi == "residual":
            acc = alpha * x_ref[...] + acc
        o_ref[...] = acc.astype(o_ref.dtype)

    part = _dot(a_ref[...], w_ref[...])
    if nk == 1:
        finish(part)
    else:
        acc_ref = scratch[0]
        k = pl.program_id(2)

        @pl.when(k == 0)
        def _():
            acc_ref[...] = part

        @pl.when(jnp.logical_and(k > 0, k < nk - 1))
        def _():
            acc_ref[...] += part

        @pl.when(k == nk - 1)
        def _():
            finish(acc_ref[...] + part)


def matmul(a, w, *, tm, tn, tk=None, epi="none", extra=None, alpha=1.0, out_dtype=F32):
    M, K = a.shape
    _, N = w.shape
    tm, tn = min(tm, M), min(tn, N)
    tk = K if tk is None else min(tk, K)
    assert M % tm == 0 and N % tn == 0 and K % tk == 0
    nk = K // tk
    in_specs = [pl.BlockSpec((tm, tk), lambda i, j, k: (i, k)),
                pl.BlockSpec((tk, tn), lambda i, j, k: (k, j))]
    args = [a, w]
    if epi == "bias_sigmoid":
        in_specs.append(pl.BlockSpec((1, tn), lambda i, j, k: (0, j)))
        args.append(extra)
    elif epi == "residual":
        in_specs.append(pl.BlockSpec((tm, tn), lambda i, j, k: (i, j)))
        args.append(extra)
    scratch = [pltpu.VMEM((tm, tn), F32)] if nk > 1 else []
    return pl.pallas_call(
        functools.partial(_mm_kernel, nk=nk, epi=epi, alpha=alpha),
        grid=(M // tm, N // tn, nk),
        in_specs=in_specs,
        out_specs=pl.BlockSpec((tm, tn), lambda i, j, k: (i, j)),
        out_shape=jax.ShapeDtypeStruct((M, N), out_dtype),
        scratch_shapes=scratch,
        compiler_params=_cparams(("parallel", "parallel", "arbitrary")),
        name="mm_" + epi,
    )(*args)


def _ln_kernel(x_ref, g_ref, b_ref, of_ref, ob_ref):
    x = x_ref[...]
    mu = jnp.mean(x, axis=-1, keepdims=True)
    xc = x - mu
    var = jnp.mean(xc * xc, axis=-1, keepdims=True)
    y = xc * lax.rsqrt(var + LN_EPS) * g_ref[...] + b_ref[...]
    of_ref[...] = y
    ob_ref[...] = y.astype(BF16)


def layer_norm(x, g, b, *, tm=256):
    M, D = x.shape
    tm = min(tm, M)
    row = pl.BlockSpec((tm, D), lambda i: (i, 0))
    vec = pl.BlockSpec((1, D), lambda i: (0, 0))
    return pl.pallas_call(
        _ln_kernel,
        grid=(M // tm,),
        in_specs=[row, vec, vec],
        out_specs=[row, row],
        out_shape=[jax.ShapeDtypeStruct((M, D), F32), jax.ShapeDtypeStruct((M, D), BF16)],
        compiler_params=_cparams(("parallel",)),
        name="layer_norm",
    )(x, g.reshape(1, D), b.reshape(1, D))


def _merge_kernel(ya_ref, yb_ref, yc_ref, w_ref, g0_ref, g1_ref, g2_ref, o_ref):
    a0, a1 = GLA_V_W, GLA_V_W + MOBA_W
    acc = g0_ref[...] * _dot(ya_ref[...], w_ref[0:a0, :])
    acc += g1_ref[...] * _dot(yb_ref[...], w_ref[a0:a1, :])
    acc += g2_ref[...] * _dot(yc_ref[...], w_ref[a1:, :])
    o_ref[...] = acc.astype(o_ref.dtype)


def gated_merge(y_gla, y_moba, y_gdn, w_branch, gates, *, tm=1024, tn=512):
    M = y_gla.shape[0]
    KW, D = w_branch.shape
    tm, tn = min(tm, M), min(tn, D)
    nj = D // tn

    def yspec(width):
        return pl.BlockSpec((tm, width), lambda i, j: (i, 0))

    def gspec(branch):
        return pl.BlockSpec((tm, tn), lambda i, j: (i, branch * nj + j))

    return pl.pallas_call(
        _merge_kernel,
        grid=(M // tm, nj),
        in_specs=[yspec(GLA_V_W), yspec(MOBA_W), yspec(GDN_V_W),
                  pl.BlockSpec((KW, tn), lambda i, j: (0, j)),
                  gspec(0), gspec(1), gspec(2)],
        out_specs=pl.BlockSpec((tm, tn), lambda i, j: (i, j)),
        out_shape=jax.ShapeDtypeStruct((M, D), BF16),
        compiler_params=_cparams(("parallel", "parallel")),
        name="gated_merge",
    )(y_gla, y_moba, y_gdn, w_branch, gates, gates, gates)


def _gla_kernel(q_ref, k_ref, v_ref, r_ref, sm_ref, wg2_ref, bg_ref, nw_ref, o_ref, st_ref, *, nc):
    C = GLA_CHUNK

    @pl.when(pl.program_id(1) == 0)
    def _():
        st_ref[...] = jnp.zeros_like(st_ref)

    glow = sm_ref[:, SM_GLOW:SM_GLOW + GLA_RANK]
    z = _dot(glow, wg2_ref[...], precision=HIGHEST) + bg_ref[...]
    log_a = _log_sigmoid(z) * (1.0 / GLA_TAU)
    ri = lax.broadcasted_iota(jnp.int32, (C, C), 0)
    ci = lax.broadcasted_iota(jnp.int32, (C, C), 1)
    causal = ci <= ri
    tri = jnp.where(causal, 1.0, 0.0).astype(F32)
    nw = nw_ref[...]
    scale = GLA_DK ** -0.5

    for c in range(nc):
        cs = slice(c * C, (c + 1) * C)
        G = _dot(tri, log_a[cs, :], precision=HIGHEST)
        g_mid = G[C // 2 - 1:C // 2, :]
        g_last = G[C - 1:C, :]
        qc = q_ref[cs, :] * scale
        kc = k_ref[cs, :]
        q_in = (qc * jnp.exp(G - g_mid)).astype(BF16)
        k_in = (kc * jnp.exp(g_mid - G)).astype(BF16)
        k_end = (kc * jnp.exp(g_last - G)).astype(BF16)
        q_dec = (qc * jnp.exp(G)).astype(BF16)
        dec = jnp.exp(g_last)
        for h in range(GLA_HEADS):
            hs = slice(h * GLA_DK, (h + 1) * GLA_DK)
            vs = slice(h * GLA_DV, (h + 1) * GLA_DV)
            A = jnp.where(causal, _dot_nt(q_in[:, hs], k_in[:, hs]), 0.0)
            v_h = v_ref[cs, vs].astype(BF16)
            st = st_ref[h]
            o = _dot(A.astype(BF16), v_h) + _dot_nt(q_dec[:, hs], st.astype(BF16))
            st_ref[h] = st * dec[:, hs] + _dot_tn(v_h, k_end[:, hs])
            r_h = r_ref[cs, vs]
            o = o * lax.rsqrt(jnp.mean(o * o, axis=-1, keepdims=True) + RMS_EPS) * nw
            o_ref[cs, vs] = (o * (r_h * _sigmoid(r_h))).astype(o_ref.dtype)


def gla_mixer(proj, small, w_g2, b_g, norm_w, *, batch, seq, ts=256):
    T = batch * seq
    ts = min(ts, seq)
    nt = seq // ts
    nc = ts // GLA_CHUNK

    def col(width, off):
        return pl.BlockSpec((ts, width), lambda b, t: (b * nt + t, off // width))

    def full(shape):
        return pl.BlockSpec(shape, lambda b, t: (0,) * len(shape))

    return pl.pallas_call(
        functools.partial(_gla_kernel, nc=nc),
        grid=(batch, nt),
        in_specs=[col(GLA_QK_W, OFF_GQ), col(GLA_QK_W, OFF_GK), col(GLA_V_W, OFF_GV), col(GLA_V_W, OFF_GR),
                  pl.BlockSpec((ts, SMALL_W), lambda b, t: (b * nt + t, 0)),
                  full((GLA_RANK, GLA_QK_W)), full((1, GLA_QK_W)), full((1, GLA_DV))],
        out_specs=pl.BlockSpec((ts, GLA_V_W), lambda b, t: (b * nt + t, 0)),
        out_shape=jax.ShapeDtypeStruct((T, GLA_V_W), BF16),
        scratch_shapes=[pltpu.VMEM((GLA_HEADS, GLA_DV, GLA_DK), F32)],
        compiler_params=_cparams(("parallel", "arbitrary")),
        name="gla_mixer",
    )(proj, proj, proj, proj, small, w_g2, b_g.reshape(1, -1), norm_w.reshape(1, -1))


def _rope_tables(seq):
    half = ROT_DIM // 2
    inv = jnp.power(ROPE_THETA, -jnp.arange(half, dtype=F32) * 2.0 / ROT_DIM)
    ang = jnp.arange(seq, dtype=jnp.int32).astype(F32)[:, None] * inv[None, :]
    cos, sin = jnp.cos(ang), jnp.sin(ang)
    pad = jnp.zeros((seq, MOBA_DH - ROT_DIM), F32)
    zero = jnp.zeros((seq, half), F32)
    t_cos = jnp.concatenate([cos, cos, pad + 1.0], axis=-1)
    t_lo = jnp.concatenate([-sin, zero, pad], axis=-1)
    t_hi = jnp.concatenate([zero, sin, pad], axis=-1)
    return t_cos, t_lo, t_hi


def _moba_prep_kernel(q_ref, k_ref, v_ref, tc_ref, tl_ref, th_ref, qo_ref, ko_ref, vo_ref, km_ref):
    half = ROT_DIM // 2
    tc, tl, th = tc_ref[...], tl_ref[...], th_ref[...]

    def rot(x):
        return x * tc + pltpu.roll(x, MOBA_DH - half, axis=1) * tl + pltpu.roll(x, half, axis=1) * th

    for h in range(MOBA_HEADS):
        hs = slice(h * MOBA_DH, (h + 1) * MOBA_DH)
        qo_ref[:, hs] = rot(q_ref[:, hs])
        kr = rot(k_ref[:, hs])
        ko_ref[:, hs] = kr.astype(BF16)
        km_ref[0, :, hs] = jnp.mean(kr, axis=0, keepdims=True)
    vo_ref[...] = v_ref[...].astype(BF16)


def _moba_attn_kernel(q_ref, k_ref, v_ref, km_ref, o_ref, m_ref, l_ref, acc_ref, sel_ref, *, topk):
    i = pl.program_id(2)
    Bk = MOBA_BLOCK
    nb = km_ref.shape[0]
    q = q_ref[...]
    gate = _dot_nt(q, km_ref[...], precision=HIGHEST)
    blk = lax.broadcasted_iota(jnp.int32, gate.shape, 1)
    gate = jnp.where(blk < i, gate, -jnp.inf)
    sel = jnp.zeros(gate.shape, F32)
    for r in range(topk):
        top = jnp.max(gate, axis=1, keepdims=True)
        idx = jnp.min(jnp.where(gate == top, blk, nb), axis=1, keepdims=True)
        hit = blk == idx
        sel = jnp.where(jnp.logical_and(hit, i > r), 1.0, sel)
        gate = jnp.where(hit, -jnp.inf, gate)
    sel_ref[...] = sel

    qs = (q * (MOBA_DH ** -0.5)).astype(BF16)
    row = lax.broadcasted_iota(jnp.int32, (Bk, Bk), 0)
    col = lax.broadcasted_iota(jnp.int32, (Bk, Bk), 1)
    own = pl.ds(pl.multiple_of(i * Bk, Bk), Bk)
    s = jnp.where(col <= row, _dot_nt(qs, k_ref[own, :]), -jnp.inf)
    m0 = jnp.max(s, axis=1, keepdims=True)
    p = jnp.exp(s - m0)
    m_ref[...] = m0
    l_ref[...] = jnp.sum(p, axis=1, keepdims=True)
    acc_ref[...] = _dot(p.astype(BF16), v_ref[own, :])

    def body(j, carry):
        picked = jnp.sum(jnp.where(blk == j, sel_ref[...], 0.0), axis=1, keepdims=True) > 0.5
        past = pl.ds(pl.multiple_of(j * Bk, Bk), Bk)
        sj = jnp.where(picked, _dot_nt(qs, k_ref[past, :]), -jnp.inf)
        m_prev = m_ref[...]
        m_new = jnp.maximum(m_prev, jnp.max(sj, axis=1, keepdims=True))
        a = jnp.exp(m_prev - m_new)
        pj = jnp.exp(sj - m_new)
        l_ref[...] = a * l_ref[...] + jnp.sum(pj, axis=1, keepdims=True)
        acc_ref[...] = a * acc_ref[...] + _dot(pj.astype(BF16), v_ref[past, :])
        m_ref[...] = m_new
        return carry

    lax.fori_loop(0, i, body, 0)
    o_ref[...] = (acc_ref[...] / l_ref[...]).astype(o_ref.dtype)


def moba_mixer(proj, *, batch, seq):
    T = batch * seq
    Bk = MOBA_BLOCK
    assert seq % Bk == 0
    nb = seq // Bk
    topk = max(1, min(MOBA_TOPK, nb - 1))
    t_cos, t_lo, t_hi = _rope_tables(seq)

    def col(off):
        return pl.BlockSpec((Bk, MOBA_W), lambda t: (t, off // MOBA_W))

    tab = pl.BlockSpec((Bk, MOBA_DH), lambda t: (t % nb, 0))
    wide = pl.BlockSpec((Bk, MOBA_W), lambda t: (t, 0))
    q_rot, k_rot, v_b, k_mean = pl.pallas_call(
        _moba_prep_kernel,
        grid=(T // Bk,),
        in_specs=[col(OFF_MQ), col(OFF_MK), col(OFF_MV), tab, tab, tab],
        out_specs=[wide, wide, wide, pl.BlockSpec((1, 1, MOBA_W), lambda t: (t, 0, 0))],
        out_shape=[jax.ShapeDtypeStruct((T, MOBA_W), F32), jax.ShapeDtypeStruct((T, MOBA_W), BF16),
                   jax.ShapeDtypeStruct((T, MOBA_W), BF16), jax.ShapeDtypeStruct((T // Bk, 1, MOBA_W), F32)],
        compiler_params=_cparams(("parallel",)),
        name="moba_prep",
    )(proj, proj, proj, t_cos, t_lo, t_hi)
    k_mean = k_mean.reshape(batch, nb, MOBA_W)

    kv = pl.BlockSpec((seq, MOBA_DH), lambda b, h, i: (b, h))
    qo = pl.BlockSpec((Bk, MOBA_DH), lambda b, h, i: (b * nb + i, h))
    return pl.pallas_call(
        functools.partial(_moba_attn_kernel, topk=topk),
        grid=(batch, MOBA_HEADS, nb),
        in_specs=[qo, kv, kv, pl.BlockSpec((None, nb, MOBA_DH), lambda b, h, i: (b, 0, h))],
        out_specs=qo,
        out_shape=jax.ShapeDtypeStruct((T, MOBA_W), BF16),
        scratch_shapes=[pltpu.VMEM((Bk, 1), F32), pltpu.VMEM((Bk, 1), F32),
                        pltpu.VMEM((Bk, MOBA_DH), F32), pltpu.VMEM((Bk, nb), F32)],
        compiler_params=_cparams(("parallel", "parallel", "arbitrary")),
        name="moba_attn",
    )(q_rot, k_rot, v_b, k_mean)


def _gdn_kernel(q_ref, k_ref, v_ref, z_ref, sm_ref, wq_ref, wk_ref, wv_ref, al_ref, dt_ref, nw_ref,
                o_ref, tail_ref, xbuf_ref, st_ref, *, nc, ts):
    C = GDN_CHUNK
    PADR = 8
    h = pl.program_id(1)

    @pl.when(pl.program_id(2) == 0)
    def _():
        tail_ref[...] = jnp.zeros_like(tail_ref)
        st_ref[...] = jnp.zeros_like(st_ref)

    def conv_silu(n, x_ref, w_ref):
        xbuf_ref[n, 0:PADR, :] = tail_ref[n]
        xbuf_ref[n, PADR:PADR + ts, :] = x_ref[...]
        acc = None
        for j in range(GDN_CONV):
            lo = PADR - (GDN_CONV - 1) + j
            term = w_ref[j:j + 1, :] * xbuf_ref[n, lo:lo + ts, :]
            acc = term if acc is None else acc + term
        tail_ref[n] = x_ref[ts - PADR:ts, :]
        return acc * _sigmoid(acc)

    q = conv_silu(0, q_ref, wq_ref)
    k = conv_silu(1, k_ref, wk_ref)
    v = conv_silu(2, v_ref, wv_ref)
    q = q * lax.rsqrt(jnp.sum(q * q, axis=-1, keepdims=True) + RMS_EPS) * (GDN_DK ** -0.5)
    k = k * lax.rsqrt(jnp.sum(k * k, axis=-1, keepdims=True) + RMS_EPS)

    sm = sm_ref[...]
    lane = lax.broadcasted_iota(jnp.int32, sm.shape, 1)
    g_all = -jnp.exp(al_ref[...]) * _softplus(sm + dt_ref[...])
    g = jnp.sum(jnp.where(lane == SM_DA + h, g_all, 0.0), axis=1, keepdims=True)
    beta = jnp.sum(jnp.where(lane == SM_DB + h, _sigmoid(sm), 0.0), axis=1, keepdims=True)

    ri = lax.broadcasted_iota(jnp.int32, (C, C), 0)
    ci = lax.broadcasted_iota(jnp.int32, (C, C), 1)
    incl = ci <= ri
    strict = ci < ri
    eye = ci == ri
    nw = nw_ref[...]

    for c in range(nc):
        cs = slice(c * C, (c + 1) * C)
        qc, kc, vc = q[cs, :], k[cs, :], v[cs, :]
        g_c, beta_c = g[cs, :], beta[cs, :]
        gam_row = jnp.sum(jnp.where(ri <= ci, g_c, 0.0), axis=0, keepdims=True)
        gam_col = jnp.sum(jnp.where(eye, gam_row, 0.0), axis=1, keepdims=True)
        gam_last = gam_col[C - 1:C, :]
        decay = jnp.where(incl, jnp.exp(jnp.where(incl, gam_col - gam_row, 0.0)), 0.0)
        kb = kc.astype(BF16)
        A = jnp.where(strict, beta_c * _dot_nt(kb, kb) * decay, 0.0)
        e_gam = jnp.exp(gam_col)
        X = jnp.concatenate([vc * beta_c, kc * (beta_c * e_gam)], axis=-1)
        X = X - _dot(A, X, precision=HIGHEST)
        P = A
        for _ in range(5):
            P = _dot(P, P, precision=HIGHEST)
            X = X + _dot(P, X, precision=HIGHEST)
        u, w = X[:, :GDN_DV], X[:, GDN_DV:]
        qk = jnp.where(incl, _dot_nt(qc.astype(BF16), kb) * decay, 0.0)
        q_dec = (qc * e_gam).astype(BF16)
        k_end = (kc * jnp.exp(gam_last - gam_col)).astype(BF16)
        st = st_ref[...]
        sb = st.astype(BF16)
        v_new = u - _dot(w.astype(BF16), sb)
        vb = v_new.astype(BF16)
        o = _dot(q_dec, sb) + _dot(qk.astype(BF16), vb)
        st_ref[...] = jnp.exp(gam_last) * st + _dot_tn(k_end, vb)
        z_c = z_ref[cs, :]
        o = o * lax.rsqrt(jnp.mean(o * o, axis=-1, keepdims=True) + RMS_EPS) * nw
        o_ref[cs, :] = (o * (z_c * _sigmoid(z_c))).astype(o_ref.dtype)


def gdn_mixer(proj, small, conv_w, a_log, dt_bias, norm_w, *, batch, seq, ts=256):
    T = batch * seq
    ts = min(ts, seq)
    nt = seq // ts
    nc = ts // GDN_CHUNK
    H = GDN_HEADS

    def col(off):
        return pl.BlockSpec((ts, LANE), lambda b, h, t: (b * nt + t, off // LANE + h))

    def wcol(off):
        return pl.BlockSpec((GDN_CONV, LANE), lambda b, h, t: (0, off // LANE + h))

    def vec():
        return pl.BlockSpec((1, LANE), lambda b, h, t: (0, 0))

    a_row = jnp.zeros((1, SMALL_W), F32).at[0, SM_DA:SM_DA + H].set(a_log.astype(F32))
    dt_row = jnp.zeros((1, SMALL_W), F32).at[0, SM_DA:SM_DA + H].set(dt_bias.astype(F32))
    return pl.pallas_call(
        functools.partial(_gdn_kernel, nc=nc, ts=ts),
        grid=(batch, H, nt),
        in_specs=[col(OFF_DQ), col(OFF_DK), col(OFF_DV), col(OFF_DZ),
                  pl.BlockSpec((ts, SMALL_W), lambda b, h, t: (b * nt + t, 0)),
                  wcol(0), wcol(GDN_K_W), wcol(2 * GDN_K_W), vec(), vec(), vec()],
        out_specs=pl.BlockSpec((ts, LANE), lambda b, h, t: (b * nt + t, h)),
        out_shape=jax.ShapeDtypeStruct((T, GDN_V_W), BF16),
        scratch_shapes=[pltpu.VMEM((3, 8, LANE), F32), pltpu.VMEM((3, ts + 8, LANE), F32),
                        pltpu.VMEM((GDN_DK, GDN_DV), F32)],
        compiler_params=_cparams(("parallel", "parallel", "arbitrary")),
        name="gdn_mixer",
    )(proj, proj, proj, proj, small, conv_w, conv_w, conv_w, a_row, dt_row, norm_w.reshape(1, -1))


def _cross_kernel(q_ref, k_ref, v_ref, o_ref):
    scale = CROSS_DH ** -0.5
    for h in range(CROSS_HEADS):
        hs = slice(h * CROSS_DH, (h + 1) * CROSS_DH)
        s = _dot_nt(q_ref[:, hs], k_ref[:, hs]) * scale
        p = jnp.exp(s - jnp.max(s, axis=1, keepdims=True))
        o = _dot(p.astype(BF16), v_ref[:, hs]) / jnp.sum(p, axis=1, keepdims=True)
        o_ref[:, hs] = o.astype(o_ref.dtype)


def cross_attention(q, k, v, *, batch, seq, mem_len, tq=512):
    T, W = q.shape
    tq = min(tq, seq)
    nt = seq // tq
    kv = pl.BlockSpec((mem_len, W), lambda b, t: (b, 0))
    qo = pl.BlockSpec((tq, W), lambda b, t: (b * nt + t, 0))
    return pl.pallas_call(
        _cross_kernel,
        grid=(batch, nt),
        in_specs=[qo, kv, kv],
        out_specs=qo,
        out_shape=jax.ShapeDtypeStruct((T, W), BF16),
        compiler_params=_cparams(("parallel", "parallel")),
        name="cross_attn",
    )(q, k, v)


def _split_w_in(w):
    o_glow = 2 * GLA_QK_W + GLA_V_W
    o_gr = o_glow + GLA_RANK
    o_mq = o_gr + GLA_V_W
    o_dq = o_mq + 3 * MOBA_W
    o_da = o_dq + 2 * GDN_K_W + GDN_V_W
    o_dz = o_da + 2 * GDN_HEADS
    main = jnp.concatenate([w[:, o_mq:o_dq], w[:, :o_glow], w[:, o_gr:o_mq], w[:, o_dq:o_da], w[:, o_dz:]],
                           axis=1).astype(BF16)
    pad = jnp.zeros((w.shape[0], SMALL_W - GLA_RANK - 2 * GDN_HEADS), w.dtype)
    small = jnp.concatenate([w[:, o_glow:o_gr], w[:, o_da:o_dz], pad], axis=1).astype(BF16)
    return main, small


def kernel(x, mem, w_in, gla_w_g2, gla_b_g, gla_norm_w, gdn_conv_w, gdn_a_log, gdn_dt_bias, gdn_norm_w, w_branch, w_gate, b_gate, w_out, ln1_g, ln1_b, w_cq, w_ck, w_cv, w_co, ln2_g, ln2_b, w_up, w_down, ln3_g, ln3_b):
    B, S, D = x.shape
    M = mem.shape[1]
    depth = w_in.shape[0]
    alpha = (2.0 * depth) ** 0.25
    T = B * S
    h = x.reshape(T, D).astype(F32)
    hb = h.astype(BF16)
    mem_b = mem.reshape(B * M, D).astype(BF16)
    big = dict(tm=1024, tn=1024)
    for l in range(depth):
        w_main, w_small = _split_w_in(w_in[l])
        proj = matmul(hb, w_main, **big)
        small = matmul(hb, w_small, **big)
        y_gla = gla_mixer(proj, small, gla_w_g2[l], gla_b_g[l], gla_norm_w[l], batch=B, seq=S)
        y_moba = moba_mixer(proj, batch=B, seq=S)
        y_gdn = gdn_mixer(proj, small, gdn_conv_w[l], gdn_a_log[l], gdn_dt_bias[l], gdn_norm_w[l],
                          batch=B, seq=S)
        gates = matmul(hb, w_gate[l].astype(BF16), epi="bias_sigmoid", extra=b_gate[l].reshape(1, -1), **big)
        merged = gated_merge(y_gla, y_moba, y_gdn, w_branch[l].astype(BF16), gates)
        pre = matmul(merged, w_out[l].astype(BF16), epi="residual", extra=h, alpha=alpha, **big)
        h, hb = layer_norm(pre, ln1_g[l], ln1_b[l])

        cq = matmul(hb, w_cq[l].astype(BF16), out_dtype=BF16, **big)
        ck = matmul(mem_b, w_ck[l].astype(BF16), out_dtype=BF16, **big)
        cv = matmul(mem_b, w_cv[l].astype(BF16), out_dtype=BF16, **big)
        att = cross_attention(cq, ck, cv, batch=B, seq=S, mem_len=M)
        pre = matmul(att, w_co[l].astype(BF16), epi="residual", extra=h, alpha=alpha, **big)
        h, hb = layer_norm(pre, ln2_g[l], ln2_b[l])

        a = matmul(hb, w_up[l].astype(BF16), epi="relu2", out_dtype=BF16, **big)
        pre = matmul(a, w_down[l].astype(BF16), epi="residual", extra=h, alpha=alpha, tk=2048, **big)
        h, hb = layer_norm(pre, ln3_g[l], ln3_b[l])
    return h.reshape(B, S, D).astype(x.dtype)
```

```python
import functools

import jax
import jax.numpy as jnp
from jax import lax
from jax.experimental import pallas as pl
from jax.experimental.pallas import tpu as pltpu

F32 = jnp.float32
BF16 = jnp.bfloat16
HIGHEST = lax.Precision.HIGHEST

GLA_HEADS, GLA_DK, GLA_DV, GLA_RANK, GLA_TAU, GLA_CHUNK = 8, 64, 128, 16, 16.0, 64
MOBA_HEADS, MOBA_DH, MOBA_BLOCK, MOBA_TOPK = 16, 128, 256, 3
GDN_HEADS, GDN_DK, GDN_DV, GDN_CONV, GDN_CHUNK = 8, 128, 128, 4, 64
CROSS_HEADS, CROSS_DH = 4, 128
ROPE_THETA = 500000.0
ROT_DIM = MOBA_DH // 4
LN_EPS = 1e-5
RMS_EPS = 1e-6

GLA_QK_W = GLA_HEADS * GLA_DK
GLA_V_W = GLA_HEADS * GLA_DV
MOBA_W = MOBA_HEADS * MOBA_DH
GDN_K_W = GDN_HEADS * GDN_DK
GDN_V_W = GDN_HEADS * GDN_DV

OFF_MQ = 0
OFF_MK = OFF_MQ + MOBA_W
OFF_MV = OFF_MK + MOBA_W
OFF_GQ = OFF_MV + MOBA_W
OFF_GK = OFF_GQ + GLA_QK_W
OFF_GV = OFF_GK + GLA_QK_W
OFF_GR = OFF_GV + GLA_V_W
OFF_DQ = OFF_GR + GLA_V_W
OFF_DK = OFF_DQ + GDN_K_W
OFF_DV = OFF_DK + GDN_K_W
OFF_DZ = OFF_DV + GDN_V_W
MAIN_W = OFF_DZ + GDN_V_W
SMALL_W = 128
SM_GLOW = 0
SM_DA = GLA_RANK
SM_DB = SM_DA + GDN_HEADS

LANE = 128
MOBA_HEADS_PER_STEP = 4
VMEM_LIMIT = 56 * 1024 * 1024


def _cparams(sem):
    return pltpu.CompilerParams(dimension_semantics=sem, vmem_limit_bytes=VMEM_LIMIT)


def _dot(a, b, **kw):
    return jnp.dot(a, b, preferred_element_type=F32, **kw)


def _dot_nt(a, b, **kw):
    return lax.dot_general(a, b, (((1,), (1,)), ((), ())), preferred_element_type=F32, **kw)


def _dot_tn(a, b, **kw):
    return lax.dot_general(a, b, (((0,), (0,)), ((), ())), preferred_element_type=F32, **kw)


def _split_bf16(x):
    hi = x.astype(BF16)
    return hi, (x - hi.astype(F32)).astype(BF16)


def _sigmoid(x):
    return 1.0 / (1.0 + jnp.exp(-x))


def _softplus(x):
    return jnp.maximum(x, 0.0) + jnp.log1p(jnp.exp(-jnp.abs(x)))


def _log_sigmoid(x):
    return jnp.minimum(x, 0.0) - jnp.log1p(jnp.exp(-jnp.abs(x)))


def _mm_kernel(*refs, nk, epi, alpha):
    if epi in ("bias_sigmoid", "residual"):
        a_ref, w_ref, x_ref, o_ref = refs[:4]
        scratch = refs[4:]
    else:
        a_ref, w_ref, o_ref = refs[:3]
        x_ref = None
        scratch = refs[3:]

    def finish(acc):
        if epi == "bias_sigmoid":
            acc = _sigmoid(acc + x_ref[...])
        elif epi == "relu2":
            r = jnp.maximum(acc, 0.0)
            acc = r * r
        elif epi == "residual":
            acc = alpha * x_ref[...] + acc
        o_ref[...] = acc.astype(o_ref.dtype)

    part = _dot(a_ref[...], w_ref[...])
    if nk == 1:
        finish(part)
    else:
        acc_ref = scratch[0]
        k = pl.program_id(2)

        @pl.when(k == 0)
        def _():
            acc_ref[...] = part

        @pl.when(jnp.logical_and(k > 0, k < nk - 1))
        def _():
            acc_ref[...] += part

        @pl.when(k == nk - 1)
        def _():
            finish(acc_ref[...] + part)


def matmul(a, w, *, tm, tn, tk=None, epi="none", extra=None, alpha=1.0, out_dtype=F32):
    M, K = a.shape
    _, N = w.shape
    tm, tn = min(tm, M), min(tn, N)
    tk = K if tk is None else min(tk, K)
    assert M % tm == 0 and N % tn == 0 and K % tk == 0
    nk = K // tk
    in_specs = [pl.BlockSpec((tm, tk), lambda i, j, k: (i, k)),
                pl.BlockSpec((tk, tn), lambda i, j, k: (k, j))]
    args = [a, w]
    if epi == "bias_sigmoid":
        in_specs.append(pl.BlockSpec((1, tn), lambda i, j, k: (0, j)))
        args.append(extra)
    elif epi == "residual":
        in_specs.append(pl.BlockSpec((tm, tn), lambda i, j, k: (i, j)))
        args.append(extra)
    scratch = [pltpu.VMEM((tm, tn), F32)] if nk > 1 else []
    return pl.pallas_call(
        functools.partial(_mm_kernel, nk=nk, epi=epi, alpha=alpha),
        grid=(M // tm, N // tn, nk),
        in_specs=in_specs,
        out_specs=pl.BlockSpec((tm, tn), lambda i, j, k: (i, j)),
        out_shape=jax.ShapeDtypeStruct((M, N), out_dtype),
        scratch_shapes=scratch,
        compiler_params=_cparams(("parallel", "parallel", "arbitrary")),
        name="mm_" + epi,
    )(*args)


def _ln_kernel(x_ref, g_ref, b_ref, of_ref, ob_ref):
    x = x_ref[...]
    mu = jnp.mean(x, axis=-1, keepdims=True)
    xc = x - mu
    var = jnp.mean(xc * xc, axis=-1, keepdims=True)
    y = xc * lax.rsqrt(var + LN_EPS) * g_ref[...] + b_ref[...]
    of_ref[...] = y
    ob_ref[...] = y.astype(BF16)


def layer_norm(x, g, b, *, tm=256):
    M, D = x.shape
    tm = min(tm, M)
    row = pl.BlockSpec((tm, D), lambda i: (i, 0))
    vec = pl.BlockSpec((1, D), lambda i: (0, 0))
    return pl.pallas_call(
        _ln_kernel,
        grid=(M // tm,),
        in_specs=[row, vec, vec],
        out_specs=[row, row],
        out_shape=[jax.ShapeDtypeStruct((M, D), F32), jax.ShapeDtypeStruct((M, D), BF16)],
        compiler_params=_cparams(("parallel",)),
        name="layer_norm",
    )(x, g.reshape(1, D), b.reshape(1, D))


def _merge_kernel(ya_ref, yb_ref, yc_ref, w_ref, g0_ref, g1_ref, g2_ref, o_ref):
    a0, a1 = GLA_V_W, GLA_V_W + MOBA_W
    acc = g0_ref[...] * _dot(ya_ref[...], w_ref[0:a0, :])
    acc += g1_ref[...] * _dot(yb_ref[...], w_ref[a0:a1, :])
    acc += g2_ref[...] * _dot(yc_ref[...], w_ref[a1:, :])
    o_ref[...] = acc.astype(o_ref.dtype)


def gated_merge(y_gla, y_moba, y_gdn, w_branch, gates, *, tm=1024, tn=512):
    M = y_gla.shape[0]
    KW, D = w_branch.shape
    tm, tn = min(tm, M), min(tn, D)
    nj = D // tn

    def yspec(width):
        return pl.BlockSpec((tm, width), lambda i, j: (i, 0))

    def gspec(branch):
        return pl.BlockSpec((tm, tn), lambda i, j: (i, branch * nj + j))

    return pl.pallas_call(
        _merge_kernel,
        grid=(M // tm, nj),
        in_specs=[yspec(GLA_V_W), yspec(MOBA_W), yspec(GDN_V_W),
                  pl.BlockSpec((KW, tn), lambda i, j: (0, j)),
                  gspec(0), gspec(1), gspec(2)],
        out_specs=pl.BlockSpec((tm, tn), lambda i, j: (i, j)),
        out_shape=jax.ShapeDtypeStruct((M, D), BF16),
        compiler_params=_cparams(("parallel", "parallel")),
        name="gated_merge",
    )(y_gla, y_moba, y_gdn, w_branch, gates, gates, gates)


def _gla_kernel(q_ref, k_ref, v_ref, r_ref, sm_ref, wg2_ref, bg_ref, nw_ref, o_ref, st_ref, *, nc):
    C = GLA_CHUNK

    @pl.when(pl.program_id(1) == 0)
    def _():
        st_ref[...] = jnp.zeros_like(st_ref)

    glow = sm_ref[:, SM_GLOW:SM_GLOW + GLA_RANK]
    z = _dot(glow, wg2_ref[...], precision=HIGHEST) + bg_ref[...]
    log_a = _log_sigmoid(z) * (1.0 / GLA_TAU)
    ri = lax.broadcasted_iota(jnp.int32, (C, C), 0)
    ci = lax.broadcasted_iota(jnp.int32, (C, C), 1)
    causal = ci <= ri
    tri = jnp.where(causal, 1.0, 0.0).astype(F32)
    nw = nw_ref[...]
    scale = GLA_DK ** -0.5

    for c in range(nc):
        cs = slice(c * C, (c + 1) * C)
        G = _dot(tri, log_a[cs, :], precision=HIGHEST)
        g_mid = G[C // 2 - 1:C // 2, :]
        g_last = G[C - 1:C, :]
        qc = q_ref[cs, :] * scale
        kc = k_ref[cs, :]
        q_in = (qc * jnp.exp(G - g_mid)).astype(BF16)
        k_in = (kc * jnp.exp(g_mid - G)).astype(BF16)
        k_end = (kc * jnp.exp(g_last - G)).astype(BF16)
        q_dec = (qc * jnp.exp(G)).astype(BF16)
        dec = jnp.exp(g_last)
        for h in range(GLA_HEADS):
            hs = slice(h * GLA_DK, (h + 1) * GLA_DK)
            vs = slice(h * GLA_DV, (h + 1) * GLA_DV)
            A = jnp.where(causal, _dot_nt(q_in[:, hs], k_in[:, hs]), 0.0)
            v_h = v_ref[cs, vs].astype(BF16)
            st = st_ref[h]
            o = _dot(A.astype(BF16), v_h) + _dot_nt(q_dec[:, hs], st.astype(BF16))
            st_ref[h] = st * dec[:, hs] + _dot_tn(v_h, k_end[:, hs])
            r_h = r_ref[cs, vs]
            o = o * lax.rsqrt(jnp.mean(o * o, axis=-1, keepdims=True) + RMS_EPS) * nw
            o_ref[cs, vs] = (o * (r_h * _sigmoid(r_h))).astype(o_ref.dtype)


def gla_mixer(proj, small, w_g2, b_g, norm_w, *, batch, seq, ts=256):
    T = batch * seq
    ts = min(ts, seq)
    nt = seq // ts
    nc = ts // GLA_CHUNK

    def col(width, off):
        return pl.BlockSpec((ts, width), lambda b, t: (b * nt + t, off // width))

    def full(shape):
        return pl.BlockSpec(shape, lambda b, t: (0,) * len(shape))

    return pl.pallas_call(
        functools.partial(_gla_kernel, nc=nc),
        grid=(batch, nt),
        in_specs=[col(GLA_QK_W, OFF_GQ), col(GLA_QK_W, OFF_GK), col(GLA_V_W, OFF_GV), col(GLA_V_W, OFF_GR),
                  pl.BlockSpec((ts, SMALL_W), lambda b, t: (b * nt + t, 0)),
                  full((GLA_RANK, GLA_QK_W)), full((1, GLA_QK_W)), full((1, GLA_DV))],
        out_specs=pl.BlockSpec((ts, GLA_V_W), lambda b, t: (b * nt + t, 0)),
        out_shape=jax.ShapeDtypeStruct((T, GLA_V_W), BF16),
        scratch_shapes=[pltpu.VMEM((GLA_HEADS, GLA_DV, GLA_DK), F32)],
        compiler_params=_cparams(("parallel", "arbitrary")),
        name="gla_mixer",
    )(proj, proj, proj, proj, small, w_g2, b_g.reshape(1, -1), norm_w.reshape(1, -1))


def _rope_tables(seq):
    half = ROT_DIM // 2
    inv = jnp.power(ROPE_THETA, -jnp.arange(half, dtype=F32) * 2.0 / ROT_DIM)
    ang = jnp.arange(seq, dtype=jnp.int32).astype(F32)[:, None] * inv[None, :]
    cos, sin = jnp.cos(ang), jnp.sin(ang)
    pad = jnp.zeros((seq, MOBA_DH - ROT_DIM), F32)
    zero = jnp.zeros((seq, half), F32)
    t_cos = jnp.concatenate([cos, cos, pad + 1.0], axis=-1)
    t_lo = jnp.concatenate([-sin, zero, pad], axis=-1)
    t_hi = jnp.concatenate([zero, sin, pad], axis=-1)
    return t_cos, t_lo, t_hi


def _moba_prep_kernel(q_ref, k_ref, v_ref, tc_ref, tl_ref, th_ref, qo_ref, ko_ref, vt_ref, km_ref):
    half = ROT_DIM // 2
    tc, tl, th = tc_ref[...], tl_ref[...], th_ref[...]

    def rot(x):
        return x * tc + pltpu.roll(x, MOBA_DH - half, axis=1) * tl + pltpu.roll(x, half, axis=1) * th

    for h in range(MOBA_HEADS):
        hs = slice(h * MOBA_DH, (h + 1) * MOBA_DH)
        qo_ref[:, hs] = rot(q_ref[:, hs])
        kr = rot(k_ref[:, hs])
        ko_ref[:, hs] = kr.astype(BF16)
        km_ref[0, :, hs] = jnp.mean(kr, axis=0, keepdims=True)
        vt_ref[0, h] = v_ref[:, hs].T.astype(BF16)


def _moba_attn_kernel(q_ref, k_ref, vt_ref, km_ref, o_ref, qs_ref, acc_ref, sel_ref, *, topk, heads):
    i = pl.program_id(2)
    Bk = MOBA_BLOCK
    nb = km_ref.shape[0]
    hslices = [slice(g * MOBA_DH, (g + 1) * MOBA_DH) for g in range(heads)]
    key = lax.broadcasted_iota(jnp.int32, (Bk, Bk), 0)
    qry = lax.broadcasted_iota(jnp.int32, (Bk, Bk), 1)
    blk = lax.broadcasted_iota(jnp.int32, (nb, Bk), 0)
    own = pl.ds(pl.multiple_of(i * Bk, Bk), Bk)

    for g, hs in enumerate(hslices):
        qs_ref[g] = (q_ref[:, hs] * (MOBA_DH ** -0.5)).astype(BF16)
    scores = [_dot_nt(k_ref[own, hs], qs_ref[g]) for g, hs in enumerate(hslices)]
    gates = [_dot_nt(km_ref[:, hs], q_ref[:, hs], precision=HIGHEST) for hs in hslices]
    for g in range(heads):
        gate = jnp.where(blk < i, gates[g], -jnp.inf)
        sel = jnp.zeros(gate.shape, F32)
        for r in range(topk):
            top = jnp.max(gate, axis=0, keepdims=True)
            idx = jnp.min(jnp.where(gate == top, blk, nb), axis=0, keepdims=True)
            hit = blk == idx
            sel = jnp.where(jnp.logical_and(hit, i > r), 1.0, sel)
            gate = jnp.where(hit, -jnp.inf, gate)
        sel_ref[g] = sel
    init, probs = [], []
    for g in range(heads):
        s = jnp.where(key <= qry, scores[g], -jnp.inf)
        m0 = jnp.max(s, axis=0, keepdims=True)
        p = jnp.exp(s - m0)
        init.append((m0, jnp.sum(p, axis=0, keepdims=True)))
        probs.append(p.astype(BF16))
    for g in range(heads):
        acc_ref[g] = _dot(vt_ref[i, g], probs[g])

    def body(j, carry):
        past = pl.ds(pl.multiple_of(j * Bk, Bk), Bk)
        sc = [_dot_nt(k_ref[past, hs], qs_ref[g]) for g, hs in enumerate(hslices)]
        out, pr, scale = [], [], []
        for g in range(heads):
            m_prev, l_prev = carry[g]
            picked = sel_ref[g, pl.ds(j, 1), :] > 0.5
            sj = jnp.where(picked, sc[g], -jnp.inf)
            m_new = jnp.maximum(m_prev, jnp.max(sj, axis=0, keepdims=True))
            a = jnp.exp(m_prev - m_new)
            pj = jnp.exp(sj - m_new)
            out.append((m_new, a * l_prev + jnp.sum(pj, axis=0, keepdims=True)))
            pr.append(pj.astype(BF16))
            scale.append(a)
        pv = [_dot(vt_ref[j, g], pr[g]) for g in range(heads)]
        for g in range(heads):
            acc_ref[g] = scale[g] * acc_ref[g] + pv[g]
        return tuple(out)

    final = lax.fori_loop(0, i, body, tuple(init))
    for g, hs in enumerate(hslices):
        o_ref[:, hs] = (acc_ref[g] / final[g][1]).T.astype(o_ref.dtype)


def moba_mixer(proj, *, batch, seq):
    T = batch * seq
    Bk = MOBA_BLOCK
    assert seq % Bk == 0
    nb = seq // Bk
    topk = max(1, min(MOBA_TOPK, nb - 1))
    t_cos, t_lo, t_hi = _rope_tables(seq)

    def col(off):
        return pl.BlockSpec((Bk, MOBA_W), lambda t: (t, off // MOBA_W))

    tab = pl.BlockSpec((Bk, MOBA_DH), lambda t: (t % nb, 0))
    wide = pl.BlockSpec((Bk, MOBA_W), lambda t: (t, 0))
    q_rot, k_rot, v_t, k_mean = pl.pallas_call(
        _moba_prep_kernel,
        grid=(T // Bk,),
        in_specs=[col(OFF_MQ), col(OFF_MK), col(OFF_MV), tab, tab, tab],
        out_specs=[wide, wide, pl.BlockSpec((1, MOBA_HEADS, MOBA_DH, Bk), lambda t: (t, 0, 0, 0)),
                   pl.BlockSpec((1, 1, MOBA_W), lambda t: (t, 0, 0))],
        out_shape=[jax.ShapeDtypeStruct((T, MOBA_W), F32), jax.ShapeDtypeStruct((T, MOBA_W), BF16),
                   jax.ShapeDtypeStruct((T // Bk, MOBA_HEADS, MOBA_DH, Bk), BF16),
                   jax.ShapeDtypeStruct((T // Bk, 1, MOBA_W), F32)],
        compiler_params=_cparams(("parallel",)),
        name="moba_prep",
    )(proj, proj, proj, t_cos, t_lo, t_hi)
    k_mean = k_mean.reshape(batch, nb, MOBA_W)
    v_t = v_t.reshape(batch, nb, MOBA_HEADS, MOBA_DH, Bk)

    G = MOBA_HEADS_PER_STEP
    GW = G * MOBA_DH
    qo = pl.BlockSpec((Bk, GW), lambda b, h, i: (b * nb + i, h))
    return pl.pallas_call(
        functools.partial(_moba_attn_kernel, topk=topk, heads=G),
        grid=(batch, MOBA_HEADS // G, nb),
        in_specs=[qo, pl.BlockSpec((seq, GW), lambda b, h, i: (b, h)),
                  pl.BlockSpec((None, nb, G, MOBA_DH, Bk), lambda b, h, i: (b, 0, h, 0, 0)),
                  pl.BlockSpec((None, nb, GW), lambda b, h, i: (b, 0, h))],
        out_specs=qo,
        out_shape=jax.ShapeDtypeStruct((T, MOBA_W), BF16),
        scratch_shapes=[pltpu.VMEM((G, Bk, MOBA_DH), BF16), pltpu.VMEM((G, MOBA_DH, Bk), F32),
                        pltpu.VMEM((G, nb, Bk), F32)],
        compiler_params=_cparams(("parallel", "parallel", "arbitrary")),
        name="moba_attn",
    )(q_rot, k_rot, v_t, k_mean)


def _gdn_kernel(q_ref, k_ref, v_ref, z_ref, sm_ref, wq_ref, wk_ref, wv_ref, al_ref, dt_ref, nw_ref,
                o_ref, tail_ref, xbuf_ref, st_ref, *, nc, ts):
    C = GDN_CHUNK
    PADR = 8
    h = pl.program_id(1)

    @pl.when(pl.program_id(2) == 0)
    def _():
        tail_ref[...] = jnp.zeros_like(tail_ref)
        st_ref[...] = jnp.zeros_like(st_ref)

    def conv_silu(n, x_ref, w_ref):
        xbuf_ref[n, 0:PADR, :] = tail_ref[n]
        xbuf_ref[n, PADR:PADR + ts, :] = x_ref[...]
        acc = None
        for j in range(GDN_CONV):
            lo = PADR - (GDN_CONV - 1) + j
            term = w_ref[j:j + 1, :] * xbuf_ref[n, lo:lo + ts, :]
            acc = term if acc is None else acc + term
        tail_ref[n] = x_ref[ts - PADR:ts, :]
        return acc * _sigmoid(acc)

    q = conv_silu(0, q_ref, wq_ref)
    k = conv_silu(1, k_ref, wk_ref)
    v = conv_silu(2, v_ref, wv_ref)
    q = q * lax.rsqrt(jnp.sum(q * q, axis=-1, keepdims=True) + RMS_EPS) * (GDN_DK ** -0.5)
    k = k * lax.rsqrt(jnp.sum(k * k, axis=-1, keepdims=True) + RMS_EPS)

    sm = sm_ref[...]
    lane = lax.broadcasted_iota(jnp.int32, sm.shape, 1)
    g_all = -jnp.exp(al_ref[...]) * _softplus(sm + dt_ref[...])
    g = jnp.sum(jnp.where(lane == SM_DA + h, g_all, 0.0), axis=1, keepdims=True)
    beta = jnp.sum(jnp.where(lane == SM_DB + h, _sigmoid(sm), 0.0), axis=1, keepdims=True)

    ri = lax.broadcasted_iota(jnp.int32, (C, C), 0)
    ci = lax.broadcasted_iota(jnp.int32, (C, C), 1)
    incl = ci <= ri
    strict = ci < ri
    eye = ci == ri
    nw = nw_ref[...]

    chunks = [slice(c * C, (c + 1) * C) for c in range(nc)]
    kb = [k[cs, :].astype(BF16) for cs in chunks]
    kk = [_dot_nt(x, x) for x in kb]
    qk_raw = [_dot_nt(q[cs, :].astype(BF16), x) for cs, x in zip(chunks, kb)]
    X, Pb, qk, q_dec, k_end, g_end = [], [], [], [], [], []
    for c, cs in enumerate(chunks):
        g_c, beta_c = g[cs, :], beta[cs, :]
        gam_row = jnp.sum(jnp.where(ri <= ci, g_c, 0.0), axis=0, keepdims=True)
        gam_col = jnp.sum(jnp.where(eye, gam_row, 0.0), axis=1, keepdims=True)
        gam_last = gam_col[C - 1:C, :]
        decay = jnp.where(incl, jnp.exp(jnp.where(incl, gam_col - gam_row, 0.0)), 0.0)
        e_gam = jnp.exp(gam_col)
        Pb.append(jnp.where(strict, beta_c * kk[c] * decay, 0.0).astype(BF16))
        X.append(jnp.concatenate([v[cs, :] * beta_c, k[cs, :] * (beta_c * e_gam)], axis=-1))
        qk.append(jnp.where(incl, qk_raw[c] * decay, 0.0).astype(BF16))
        q_dec.append((q[cs, :] * e_gam).astype(BF16))
        k_end.append((k[cs, :] * jnp.exp(gam_last - gam_col)).astype(BF16))
        g_end.append(jnp.exp(gam_last))

    def apply(P, x, sign):
        hi, lo = _split_bf16(x)
        return x + sign * (_dot(P, hi) + _dot(P, lo))

    X = [apply(P, x, -1.0) for P, x in zip(Pb, X)]
    for _ in range(5):
        Pb = [_dot(P, P).astype(BF16) for P in Pb]
        X = [apply(P, x, 1.0) for P, x in zip(Pb, X)]

    outs = []
    for c in range(nc):
        u, w = X[c][:, :GDN_DV], X[c][:, GDN_DV:]
        st = st_ref[...]
        sb = st.astype(BF16)
        v_new = u - _dot(w.astype(BF16), sb)
        vb = v_new.astype(BF16)
        outs.append(_dot(q_dec[c], sb) + _dot(qk[c], vb))
        st_ref[...] = g_end[c] * st + _dot_tn(k_end[c], vb)
    for cs, o in zip(chunks, outs):
        z_c = z_ref[cs, :]
        o = o * lax.rsqrt(jnp.mean(o * o, axis=-1, keepdims=True) + RMS_EPS) * nw
        o_ref[cs, :] = (o * (z_c * _sigmoid(z_c))).astype(o_ref.dtype)


def gdn_mixer(proj, small, conv_w, a_log, dt_bias, norm_w, *, batch, seq, ts=512):
    T = batch * seq
    ts = min(ts, seq)
    nt = seq // ts
    nc = ts // GDN_CHUNK
    H = GDN_HEADS

    def col(off):
        return pl.BlockSpec((ts, LANE), lambda b, h, t: (b * nt + t, off // LANE + h))

    def wcol(off):
        return pl.BlockSpec((GDN_CONV, LANE), lambda b, h, t: (0, off // LANE + h))

    def vec():
        return pl.BlockSpec((1, LANE), lambda b, h, t: (0, 0))

    a_row = jnp.zeros((1, SMALL_W), F32).at[0, SM_DA:SM_DA + H].set(a_log.astype(F32))
    dt_row = jnp.zeros((1, SMALL_W), F32).at[0, SM_DA:SM_DA + H].set(dt_bias.astype(F32))
    return pl.pallas_call(
        functools.partial(_gdn_kernel, nc=nc, ts=ts),
        grid=(batch, H, nt),
        in_specs=[col(OFF_DQ), col(OFF_DK), col(OFF_DV), col(OFF_DZ),
                  pl.BlockSpec((ts, SMALL_W), lambda b, h, t: (b * nt + t, 0)),
                  wcol(0), wcol(GDN_K_W), wcol(2 * GDN_K_W), vec(), vec(), vec()],
        out_specs=pl.BlockSpec((ts, LANE), lambda b, h, t: (b * nt + t, h)),
        out_shape=jax.ShapeDtypeStruct((T, GDN_V_W), BF16),
        scratch_shapes=[pltpu.VMEM((3, 8, LANE), F32), pltpu.VMEM((3, ts + 8, LANE), F32),
                        pltpu.VMEM((GDN_DK, GDN_DV), F32)],
        compiler_params=_cparams(("parallel", "parallel", "arbitrary")),
        name="gdn_mixer",
    )(proj, proj, proj, proj, small, conv_w, conv_w, conv_w, a_row, dt_row, norm_w.reshape(1, -1))


def _cross_kernel(q_ref, k_ref, v_ref, o_ref):
    scale = CROSS_DH ** -0.5
    for h in range(CROSS_HEADS):
        hs = slice(h * CROSS_DH, (h + 1) * CROSS_DH)
        s = _dot_nt(q_ref[:, hs], k_ref[:, hs]) * scale
        p = jnp.exp(s - jnp.max(s, axis=1, keepdims=True))
        o = _dot(p.astype(BF16), v_ref[:, hs]) / jnp.sum(p, axis=1, keepdims=True)
        o_ref[:, hs] = o.astype(o_ref.dtype)


def cross_attention(q, k, v, *, batch, seq, mem_len, tq=512):
    T, W = q.shape
    tq = min(tq, seq)
    nt = seq // tq
    kv = pl.BlockSpec((mem_len, W), lambda b, t: (b, 0))
    qo = pl.BlockSpec((tq, W), lambda b, t: (b * nt + t, 0))
    return pl.pallas_call(
        _cross_kernel,
        grid=(batch, nt),
        in_specs=[qo, kv, kv],
        out_specs=qo,
        out_shape=jax.ShapeDtypeStruct((T, W), BF16),
        compiler_params=_cparams(("parallel", "parallel")),
        name="cross_attn",
    )(q, k, v)


def _split_w_in(w):
    o_glow = 2 * GLA_QK_W + GLA_V_W
    o_gr = o_glow + GLA_RANK
    o_mq = o_gr + GLA_V_W
    o_dq = o_mq + 3 * MOBA_W
    o_da = o_dq + 2 * GDN_K_W + GDN_V_W
    o_dz = o_da + 2 * GDN_HEADS
    main = jnp.concatenate([w[:, o_mq:o_dq], w[:, :o_glow], w[:, o_gr:o_mq], w[:, o_dq:o_da], w[:, o_dz:]],
                           axis=1).astype(BF16)
    pad = jnp.zeros((w.shape[0], SMALL_W - GLA_RANK - 2 * GDN_HEADS), w.dtype)
    small = jnp.concatenate([w[:, o_glow:o_gr], w[:, o_da:o_dz], pad], axis=1).astype(BF16)
    return main, small


def kernel(x, mem, w_in, gla_w_g2, gla_b_g, gla_norm_w, gdn_conv_w, gdn_a_log, gdn_dt_bias, gdn_norm_w, w_branch, w_gate, b_gate, w_out, ln1_g, ln1_b, w_cq, w_ck, w_cv, w_co, ln2_g, ln2_b, w_up, w_down, ln3_g, ln3_b):
    B, S, D = x.shape
    M = mem.shape[1]
    depth = w_in.shape[0]
    alpha = (2.0 * depth) ** 0.25
    T = B * S
    h = x.reshape(T, D).astype(F32)
    hb = h.astype(BF16)
    mem_b = mem.reshape(B * M, D).astype(BF16)
    big = dict(tm=1024, tn=1024)
    for l in range(depth):
        w_main, w_small = _split_w_in(w_in[l])
        proj = matmul(hb, w_main, **big)
        small = matmul(hb, w_small, **big)
        y_gla = gla_mixer(proj, small, gla_w_g2[l], gla_b_g[l], gla_norm_w[l], batch=B, seq=S)
        y_moba = moba_mixer(proj, batch=B, seq=S)
        y_gdn = gdn_mixer(proj, small, gdn_conv_w[l], gdn_a_log[l], gdn_dt_bias[l], gdn_norm_w[l],
                          batch=B, seq=S)
        gates = matmul(hb, w_gate[l].astype(BF16), epi="bias_sigmoid", extra=b_gate[l].reshape(1, -1), **big)
        merged = gated_merge(y_gla, y_moba, y_gdn, w_branch[l].astype(BF16), gates)
        pre = matmul(merged, w_out[l].astype(BF16), epi="residual", extra=h, alpha=alpha, **big)
        h, hb = layer_norm(pre, ln1_g[l], ln1_b[l])

        cq = matmul(hb, w_cq[l].astype(BF16), out_dtype=BF16, **big)
        ck = matmul(mem_b, w_ck[l].astype(BF16), out_dtype=BF16, **big)
        cv = matmul(mem_b, w_cv[l].astype(BF16), out_dtype=BF16, **big)
        att = cross_attention(cq, ck, cv, batch=B, seq=S, mem_len=M)
        pre = matmul(att, w_co[l].astype(BF16), epi="residual", extra=h, alpha=alpha, **big)
        h, hb = layer_norm(pre, ln2_g[l], ln2_b[l])

        a = matmul(hb, w_up[l].astype(BF16), epi="relu2", out_dtype=BF16, **big)
        pre = matmul(a, w_down[l].astype(BF16), epi="residual", extra=h, alpha=alpha, tk=2048, **big)
        h, hb = layer_norm(pre, ln3_g[l], ln3_b[l])
    return h.reshape(B, S, D).astype(x.dtype)
```

```python
import functools

import jax
import jax.numpy as jnp
from jax import lax
from jax.experimental import pallas as pl
from jax.experimental.pallas import tpu as pltpu

F32 = jnp.float32
BF16 = jnp.bfloat16
HIGHEST = lax.Precision.HIGHEST

GLA_HEADS, GLA_DK, GLA_DV, GLA_RANK, GLA_TAU, GLA_CHUNK = 8, 64, 128, 16, 16.0, 64
MOBA_HEADS, MOBA_DH, MOBA_BLOCK, MOBA_TOPK = 16, 128, 256, 3
GDN_HEADS, GDN_DK, GDN_DV, GDN_CONV, GDN_CHUNK = 8, 128, 128, 4, 64
CROSS_HEADS, CROSS_DH = 4, 128
ROPE_THETA = 500000.0
ROT_DIM = MOBA_DH // 4
LN_EPS = 1e-5
RMS_EPS = 1e-6

GLA_QK_W = GLA_HEADS * GLA_DK
GLA_V_W = GLA_HEADS * GLA_DV
MOBA_W = MOBA_HEADS * MOBA_DH
GDN_K_W = GDN_HEADS * GDN_DK
GDN_V_W = GDN_HEADS * GDN_DV

OFF_MQ = 0
OFF_MK = OFF_MQ + MOBA_W
OFF_MV = OFF_MK + MOBA_W
OFF_GQ = OFF_MV + MOBA_W
OFF_GK = OFF_GQ + GLA_QK_W
OFF_GV = OFF_GK + GLA_QK_W
OFF_GR = OFF_GV + GLA_V_W
OFF_DQ = OFF_GR + GLA_V_W
OFF_DK = OFF_DQ + GDN_K_W
OFF_DV = OFF_DK + GDN_K_W
OFF_DZ = OFF_DV + GDN_V_W
MAIN_W = OFF_DZ + GDN_V_W
SMALL_W = 128
SM_GLOW = 0
SM_DA = GLA_RANK
SM_DB = SM_DA + GDN_HEADS

LANE = 128
MOBA_HEADS_PER_STEP = 4
MOBA_VT_ROWS = MOBA_DH + 8
LOG2_E = 1.4426950408889634
MOBA_BLOCKS_PER_TRIP = (2, 1)
VMEM_LIMIT = 56 * 1024 * 1024


def _cparams(sem):
    return pltpu.CompilerParams(dimension_semantics=sem, vmem_limit_bytes=VMEM_LIMIT)


def _dot(a, b, **kw):
    return jnp.dot(a, b, preferred_element_type=F32, **kw)


def _dot_nt(a, b, **kw):
    return lax.dot_general(a, b, (((1,), (1,)), ((), ())), preferred_element_type=F32, **kw)


def _dot_tn(a, b, **kw):
    return lax.dot_general(a, b, (((0,), (0,)), ((), ())), preferred_element_type=F32, **kw)


def _split_bf16(x):
    hi = x.astype(BF16)
    return hi, (x - hi.astype(F32)).astype(BF16)


def _sigmoid(x):
    return 1.0 / (1.0 + jnp.exp(-x))


def _softplus(x):
    return jnp.maximum(x, 0.0) + jnp.log1p(jnp.exp(-jnp.abs(x)))


def _log_sigmoid(x):
    return jnp.minimum(x, 0.0) - jnp.log1p(jnp.exp(-jnp.abs(x)))


def _mm_kernel(*refs, nk, epi, alpha):
    if epi in ("bias_sigmoid", "residual"):
        a_ref, w_ref, x_ref, o_ref = refs
    else:
        a_ref, w_ref, o_ref = refs
        x_ref = None

    def finish(acc):
        if epi == "bias_sigmoid":
            acc = _sigmoid(acc + x_ref[...])
        elif epi == "relu2":
            r = jnp.maximum(acc, 0.0)
            acc = r * r
        elif epi == "residual":
            acc = alpha * x_ref[...] + acc
        o_ref[...] = acc.astype(o_ref.dtype)

    part = _dot(a_ref[...], w_ref[...])
    if nk == 1:
        finish(part)
    else:
        k = pl.program_id(2)

        @pl.when(k == 0)
        def _():
            o_ref[...] = part

        @pl.when(jnp.logical_and(k > 0, k < nk - 1))
        def _():
            o_ref[...] += part

        @pl.when(k == nk - 1)
        def _():
            finish(o_ref[...] + part)


def matmul(a, w, *, tm, tn, tk=None, epi="none", extra=None, alpha=1.0, out_dtype=F32):
    M, K = a.shape
    _, N = w.shape
    tm, tn = min(tm, M), min(tn, N)
    tk = K if tk is None else min(tk, K)
    assert M % tm == 0 and N % tn == 0 and K % tk == 0
    nk = K // tk
    assert nk == 1 or out_dtype == F32
    in_specs = [pl.BlockSpec((tm, tk), lambda i, j, k: (i, k)),
                pl.BlockSpec((tk, tn), lambda i, j, k: (k, j))]
    args = [a, w]
    if epi == "bias_sigmoid":
        in_specs.append(pl.BlockSpec((1, tn), lambda i, j, k: (0, j)))
        args.append(extra)
    elif epi == "residual":
        mode = dict(pipeline_mode=pl.Buffered(1)) if nk > 1 else {}
        in_specs.append(pl.BlockSpec((tm, tn), lambda i, j, k: (i, j), **mode))
        args.append(extra)
    return pl.pallas_call(
        functools.partial(_mm_kernel, nk=nk, epi=epi, alpha=alpha),
        grid=(M // tm, N // tn, nk),
        in_specs=in_specs,
        out_specs=pl.BlockSpec((tm, tn), lambda i, j, k: (i, j)),
        out_shape=jax.ShapeDtypeStruct((M, N), out_dtype),
        compiler_params=_cparams(("parallel", "parallel", "arbitrary")),
        name="mm_" + epi,
    )(*args)


CAST_BLOCK_BYTES = 8 * 1024 * 1024


def _cast_kernel(x_ref, o_ref):
    o_ref[...] = x_ref[...].astype(o_ref.dtype)


def cast_bf16(w):
    R, N = w.shape
    tr = 8
    while tr * 2 <= min(R, CAST_BLOCK_BYTES // (4 * N)):
        tr *= 2
    assert R % tr == 0
    spec = pl.BlockSpec((tr, N), lambda i: (i, 0))
    return pl.pallas_call(
        _cast_kernel, grid=(R // tr,), in_specs=[spec], out_specs=spec,
        out_shape=jax.ShapeDtypeStruct((R, N), BF16),
        compiler_params=_cparams(("parallel",)), name="cast_bf16",
    )(w)


def _split_w_in_kernel(x_ref, main_ref, small_ref):
    o_glow = 2 * GLA_QK_W + GLA_V_W
    o_gr = o_glow + GLA_RANK
    o_mq = o_gr + GLA_V_W
    o_dq = o_mq + 3 * MOBA_W
    o_da = o_dq + 2 * GDN_K_W + GDN_V_W
    o_dz = o_da + 2 * GDN_HEADS
    pos = 0
    for lo, hi in ((o_mq, o_dq), (0, o_glow), (o_gr, o_mq), (o_dq, o_da), (o_dz, o_dz + GDN_V_W)):
        main_ref[:, pos:pos + hi - lo] = x_ref[:, lo:hi].astype(BF16)
        pos += hi - lo
    small_ref[...] = jnp.zeros_like(small_ref)
    small_ref[:, SM_GLOW:SM_GLOW + GLA_RANK] = x_ref[:, o_glow:o_gr].astype(BF16)
    small_ref[:, SM_DA:SM_DA + 2 * GDN_HEADS] = x_ref[:, o_da:o_dz].astype(BF16)


def split_w_in(w):
    R, N = w.shape
    tr = 128
    return pl.pallas_call(
        _split_w_in_kernel, grid=(R // tr,),
        in_specs=[pl.BlockSpec((tr, N), lambda i: (i, 0))],
        out_specs=[pl.BlockSpec((tr, MAIN_W), lambda i: (i, 0)), pl.BlockSpec((tr, SMALL_W), lambda i: (i, 0))],
        out_shape=[jax.ShapeDtypeStruct((R, MAIN_W), BF16), jax.ShapeDtypeStruct((R, SMALL_W), BF16)],
        compiler_params=_cparams(("parallel",)), name="split_w_in",
    )(w)


def _ln_kernel(x_ref, g_ref, b_ref, of_ref, ob_ref):
    x = x_ref[...]
    mu = jnp.mean(x, axis=-1, keepdims=True)
    xc = x - mu
    var = jnp.mean(xc * xc, axis=-1, keepdims=True)
    y = xc * lax.rsqrt(var + LN_EPS) * g_ref[...] + b_ref[...]
    of_ref[...] = y
    ob_ref[...] = y.astype(BF16)


def layer_norm(x, g, b, *, tm=256):
    M, D = x.shape
    tm = min(tm, M)
    row = pl.BlockSpec((tm, D), lambda i: (i, 0))
    vec = pl.BlockSpec((1, D), lambda i: (0, 0))
    return pl.pallas_call(
        _ln_kernel,
        grid=(M // tm,),
        in_specs=[row, vec, vec],
        out_specs=[row, row],
        out_shape=[jax.ShapeDtypeStruct((M, D), F32), jax.ShapeDtypeStruct((M, D), BF16)],
        compiler_params=_cparams(("parallel",)),
        name="layer_norm",
    )(x, g.reshape(1, D), b.reshape(1, D))


def _merge_kernel(ya_ref, yb_ref, yc_ref, w_ref, g0_ref, g1_ref, g2_ref, o_ref):
    a0, a1 = GLA_V_W, GLA_V_W + MOBA_W
    acc = g0_ref[...] * _dot(ya_ref[...], w_ref[0:a0, :])
    acc += g1_ref[...] * _dot(yb_ref[...], w_ref[a0:a1, :])
    acc += g2_ref[...] * _dot(yc_ref[...], w_ref[a1:, :])
    o_ref[...] = acc.astype(o_ref.dtype)


def gated_merge(y_gla, y_moba, y_gdn, w_branch, gates, *, tm=1024, tn=512):
    M = y_gla.shape[0]
    KW, D = w_branch.shape
    tm, tn = min(tm, M), min(tn, D)
    nj = D // tn

    def yspec(width):
        return pl.BlockSpec((tm, width), lambda i, j: (i, 0))

    def gspec(branch):
        return pl.BlockSpec((tm, tn), lambda i, j: (i, branch * nj + j))

    return pl.pallas_call(
        _merge_kernel,
        grid=(M // tm, nj),
        in_specs=[yspec(GLA_V_W), yspec(MOBA_W), yspec(GDN_V_W),
                  pl.BlockSpec((KW, tn), lambda i, j: (0, j)),
                  gspec(0), gspec(1), gspec(2)],
        out_specs=pl.BlockSpec((tm, tn), lambda i, j: (i, j)),
        out_shape=jax.ShapeDtypeStruct((M, D), BF16),
        compiler_params=_cparams(("parallel", "parallel")),
        name="gated_merge",
    )(y_gla, y_moba, y_gdn, w_branch, gates, gates, gates)


def _gla_kernel(q_ref, k_ref, v_ref, r_ref, sm_ref, wg2_ref, bg_ref, nw_ref, o_ref, st_ref, *, nc):
    C = GLA_CHUNK

    @pl.when(pl.program_id(1) == 0)
    def _():
        st_ref[...] = jnp.zeros_like(st_ref)

    glow = sm_ref[:, SM_GLOW:SM_GLOW + GLA_RANK]
    z = _dot(glow, wg2_ref[...], precision=HIGHEST) + bg_ref[...]
    log_a = _log_sigmoid(z) * (1.0 / GLA_TAU)
    ri = lax.broadcasted_iota(jnp.int32, (C, C), 0)
    ci = lax.broadcasted_iota(jnp.int32, (C, C), 1)
    causal = ci <= ri
    tri = jnp.where(causal, 1.0, 0.0).astype(F32)
    nw = nw_ref[...]
    scale = GLA_DK ** -0.5

    for c in range(nc):
        cs = slice(c * C, (c + 1) * C)
        G = _dot(tri, log_a[cs, :], precision=HIGHEST)
        g_mid = G[C // 2 - 1:C // 2, :]
        g_last = G[C - 1:C, :]
        qc = q_ref[cs, :] * scale
        kc = k_ref[cs, :]
        q_in = (qc * jnp.exp(G - g_mid)).astype(BF16)
        k_in = (kc * jnp.exp(g_mid - G)).astype(BF16)
        k_end = (kc * jnp.exp(g_last - G)).astype(BF16)
        q_dec = (qc * jnp.exp(G)).astype(BF16)
        dec = jnp.exp(g_last)
        heads = [(h, slice(h * GLA_DK, (h + 1) * GLA_DK), slice(h * GLA_DV, (h + 1) * GLA_DV))
                 for h in range(GLA_HEADS)]
        A = [_dot_nt(q_in[:, hs], k_in[:, hs]) for _, hs, _ in heads]
        v_b = [v_ref[cs, vs].astype(BF16) for _, _, vs in heads]
        st = [st_ref[h] for h, _, _ in heads]
        o_inter = [_dot_nt(q_dec[:, hs], st[h].astype(BF16)) for h, hs, _ in heads]
        o_intra = [_dot(jnp.where(causal, A[h], 0.0).astype(BF16), v_b[h]) for h, _, _ in heads]
        upd = [_dot_tn(v_b[h], k_end[:, hs]) for h, hs, _ in heads]
        for h, hs, vs in heads:
            st_ref[h] = st[h] * dec[:, hs] + upd[h]
            o = o_intra[h] + o_inter[h]
            r_h = r_ref[cs, vs]
            o = o * lax.rsqrt(jnp.mean(o * o, axis=-1, keepdims=True) + RMS_EPS) * nw
            o_ref[cs, vs] = (o * (r_h * _sigmoid(r_h))).astype(o_ref.dtype)


def gla_mixer(proj, small, w_g2, b_g, norm_w, *, batch, seq, ts=256):
    T = batch * seq
    ts = min(ts, seq)
    nt = seq // ts
    nc = ts // GLA_CHUNK

    def col(width, off):
        return pl.BlockSpec((ts, width), lambda b, t: (b * nt + t, off // width))

    def full(shape):
        return pl.BlockSpec(shape, lambda b, t: (0,) * len(shape))

    return pl.pallas_call(
        functools.partial(_gla_kernel, nc=nc),
        grid=(batch, nt),
        in_specs=[col(GLA_QK_W, OFF_GQ), col(GLA_QK_W, OFF_GK), col(GLA_V_W, OFF_GV), col(GLA_V_W, OFF_GR),
                  pl.BlockSpec((ts, SMALL_W), lambda b, t: (b * nt + t, 0)),
                  full((GLA_RANK, GLA_QK_W)), full((1, GLA_QK_W)), full((1, GLA_DV))],
        out_specs=pl.BlockSpec((ts, GLA_V_W), lambda b, t: (b * nt + t, 0)),
        out_shape=jax.ShapeDtypeStruct((T, GLA_V_W), BF16),
        scratch_shapes=[pltpu.VMEM((GLA_HEADS, GLA_DV, GLA_DK), F32)],
        compiler_params=_cparams(("parallel", "arbitrary")),
        name="gla_mixer",
    )(proj, proj, proj, proj, small, w_g2, b_g.reshape(1, -1), norm_w.reshape(1, -1))


def _rope_tables(seq):
    half = ROT_DIM // 2
    inv = jnp.power(ROPE_THETA, -jnp.arange(half, dtype=F32) * 2.0 / ROT_DIM)
    ang = jnp.arange(seq, dtype=jnp.int32).astype(F32)[:, None] * inv[None, :]
    cos, sin = jnp.cos(ang), jnp.sin(ang)
    pad = jnp.zeros((seq, MOBA_DH - ROT_DIM), F32)
    zero = jnp.zeros((seq, half), F32)
    t_cos = jnp.concatenate([cos, cos, pad + 1.0], axis=-1)
    t_lo = jnp.concatenate([-sin, zero, pad], axis=-1)
    t_hi = jnp.concatenate([zero, sin, pad], axis=-1)
    return t_cos, t_lo, t_hi


def _moba_prep_kernel(q_ref, k_ref, v_ref, tc_ref, tl_ref, th_ref, qo_ref, ko_ref, vt_ref, km_ref):
    half = ROT_DIM // 2
    tc, tl, th = tc_ref[...], tl_ref[...], th_ref[...]

    def rot(x):
        return x * tc + pltpu.roll(x, MOBA_DH - half, axis=1) * tl + pltpu.roll(x, half, axis=1) * th

    for h in range(MOBA_HEADS):
        hs = slice(h * MOBA_DH, (h + 1) * MOBA_DH)
        qo_ref[:, hs] = rot(q_ref[:, hs])
        kr = rot(k_ref[:, hs])
        ko_ref[:, hs] = kr.astype(BF16)
        km_ref[0, :, hs] = jnp.mean(kr, axis=0, keepdims=True)
        vt_ref[0, h, 0:MOBA_DH, :] = v_ref[:, hs].T.astype(BF16)
        vt_ref[0, h, MOBA_DH:, :] = jnp.ones((MOBA_VT_ROWS - MOBA_DH, MOBA_BLOCK), BF16)


def _moba_attn_kernel(q_ref, k_ref, vt_ref, km_ref, o_ref, qs_ref, acc_ref, sel_ref, *, topk, heads):
    i = pl.program_id(2)
    Bk = MOBA_BLOCK
    nb = km_ref.shape[0]
    hslices = [slice(g * MOBA_DH, (g + 1) * MOBA_DH) for g in range(heads)]
    key = lax.broadcasted_iota(jnp.int32, (Bk, Bk), 0)
    qry = lax.broadcasted_iota(jnp.int32, (Bk, Bk), 1)
    blk = lax.broadcasted_iota(jnp.int32, (nb, Bk), 0)
    own = pl.ds(pl.multiple_of(i * Bk, Bk), Bk)

    for g, hs in enumerate(hslices):
        qs_ref[g] = (q_ref[:, hs] * (MOBA_DH ** -0.5 * LOG2_E)).astype(BF16)
    scores = [_dot_nt(k_ref[own, hs], qs_ref[g]) for g, hs in enumerate(hslices)]
    gates = [_dot_nt(km_ref[:, hs], q_ref[:, hs], precision=HIGHEST) for hs in hslices]
    for g in range(heads):
        gate = jnp.where(blk < i, gates[g], -jnp.inf)
        sel = jnp.zeros(gate.shape, F32)
        for r in range(topk):
            top = jnp.max(gate, axis=0, keepdims=True)
            idx = jnp.min(jnp.where(gate == top, blk, nb), axis=0, keepdims=True)
            hit = blk == idx
            sel = jnp.where(jnp.logical_and(hit, i > r), 1.0, sel)
            gate = jnp.where(hit, -jnp.inf, gate)
        sel_ref[g] = sel
    init, probs = [], []
    for g in range(heads):
        s = jnp.where(key <= qry, scores[g], -jnp.inf)
        m0 = jnp.max(s, axis=0, keepdims=True)
        init.append(m0)
        probs.append(jnp.exp2(s - m0).astype(BF16))
    for g in range(heads):
        acc_ref[g] = _dot(vt_ref[i, g], probs[g])

    def visit(blocks, carry):
        sc = [[_dot_nt(k_ref[pl.ds(pl.multiple_of(j * Bk, Bk), Bk), hs], qs_ref[g]) for g, hs in enumerate(hslices)]
              for j in blocks]
        m_run = list(carry)
        pr, scale = [], []
        for n, j in enumerate(blocks):
            pr.append([])
            scale.append([])
            for g in range(heads):
                picked = sel_ref[g, pl.ds(j, 1), :] > 0.5
                m_blk = jnp.where(picked, jnp.max(sc[n][g], axis=0, keepdims=True), -jnp.inf)
                m_new = jnp.maximum(m_run[g], m_blk)
                pr[n].append(jnp.exp2(sc[n][g] - jnp.where(picked, m_new, jnp.inf)).astype(BF16))
                scale[n].append(jnp.exp2(m_run[g] - m_new))
                m_run[g] = m_new
        pv = [[_dot(vt_ref[j, g], pr[n][g]) for g in range(heads)] for n, j in enumerate(blocks)]
        for g in range(heads):
            acc = acc_ref[g]
            for n in range(len(blocks)):
                acc = scale[n][g] * acc + pv[n][g]
            acc_ref[g] = acc
        return tuple(m_run)

    carry, done = tuple(init), 0
    for width in MOBA_BLOCKS_PER_TRIP:
        trips = (i - done) // width
        carry = lax.fori_loop(0, trips, lambda t, c, w=width, d=done: visit([d + t * w + n for n in range(w)], c), carry)
        done = done + trips * width
    for g, hs in enumerate(hslices):
        acc = acc_ref[g]
        o_ref[:, hs] = (acc[0:MOBA_DH, :] / acc[MOBA_DH:MOBA_DH + 1, :]).T.astype(o_ref.dtype)


def moba_mixer(proj, *, batch, seq):
    T = batch * seq
    Bk = MOBA_BLOCK
    assert seq % Bk == 0
    nb = seq // Bk
    topk = max(1, min(MOBA_TOPK, nb - 1))
    t_cos, t_lo, t_hi = _rope_tables(seq)

    def col(off):
        return pl.BlockSpec((Bk, MOBA_W), lambda t: (t, off // MOBA_W))

    tab = pl.BlockSpec((Bk, MOBA_DH), lambda t: (t % nb, 0))
    wide = pl.BlockSpec((Bk, MOBA_W), lambda t: (t, 0))
    q_rot, k_rot, v_t, k_mean = pl.pallas_call(
        _moba_prep_kernel,
        grid=(T // Bk,),
        in_specs=[col(OFF_MQ), col(OFF_MK), col(OFF_MV), tab, tab, tab],
        out_specs=[wide, wide, pl.BlockSpec((1, MOBA_HEADS, MOBA_VT_ROWS, Bk), lambda t: (t, 0, 0, 0)),
                   pl.BlockSpec((1, 1, MOBA_W), lambda t: (t, 0, 0))],
        out_shape=[jax.ShapeDtypeStruct((T, MOBA_W), F32), jax.ShapeDtypeStruct((T, MOBA_W), BF16),
                   jax.ShapeDtypeStruct((T // Bk, MOBA_HEADS, MOBA_VT_ROWS, Bk), BF16),
                   jax.ShapeDtypeStruct((T // Bk, 1, MOBA_W), F32)],
        compiler_params=_cparams(("parallel",)),
        name="moba_prep",
    )(proj, proj, proj, t_cos, t_lo, t_hi)
    k_mean = k_mean.reshape(batch, nb, MOBA_W)
    v_t = v_t.reshape(batch, nb, MOBA_HEADS, MOBA_VT_ROWS, Bk)

    G = MOBA_HEADS_PER_STEP
    GW = G * MOBA_DH
    qo = pl.BlockSpec((Bk, GW), lambda b, h, i: (b * nb + i, h))
    return pl.pallas_call(
        functools.partial(_moba_attn_kernel, topk=topk, heads=G),
        grid=(batch, MOBA_HEADS // G, nb),
        in_specs=[qo, pl.BlockSpec((seq, GW), lambda b, h, i: (b, h)),
                  pl.BlockSpec((None, nb, G, MOBA_VT_ROWS, Bk), lambda b, h, i: (b, 0, h, 0, 0)),
                  pl.BlockSpec((None, nb, GW), lambda b, h, i: (b, 0, h))],
        out_specs=qo,
        out_shape=jax.ShapeDtypeStruct((T, MOBA_W), BF16),
        scratch_shapes=[pltpu.VMEM((G, Bk, MOBA_DH), BF16), pltpu.VMEM((G, MOBA_VT_ROWS, Bk), F32),
                        pltpu.VMEM((G, nb, Bk), F32)],
        compiler_params=_cparams(("parallel", "parallel", "arbitrary")),
        name="moba_attn",
    )(q_rot, k_rot, v_t, k_mean)


def _gdn_kernel(q_ref, k_ref, v_ref, z_ref, sm_ref, wq_ref, wk_ref, wv_ref, al_ref, dt_ref, nw_ref,
                o_ref, tail_ref, xbuf_ref, st_ref, *, nc, ts):
    C = GDN_CHUNK
    PADR = 8
    h = pl.program_id(1)

    @pl.when(pl.program_id(2) == 0)
    def _():
        tail_ref[...] = jnp.zeros_like(tail_ref)
        st_ref[...] = jnp.zeros_like(st_ref)

    def conv_silu(n, x_ref, w_ref):
        xbuf_ref[n, 0:PADR, :] = tail_ref[n]
        xbuf_ref[n, PADR:PADR + ts, :] = x_ref[...]
        acc = None
        for j in range(GDN_CONV):
            lo = PADR - (GDN_CONV - 1) + j
            term = w_ref[j:j + 1, :] * xbuf_ref[n, lo:lo + ts, :]
            acc = term if acc is None else acc + term
        tail_ref[n] = x_ref[ts - PADR:ts, :]
        return acc * _sigmoid(acc)

    q = conv_silu(0, q_ref, wq_ref)
    k = conv_silu(1, k_ref, wk_ref)
    v = conv_silu(2, v_ref, wv_ref)
    q = q * lax.rsqrt(jnp.sum(q * q, axis=-1, keepdims=True) + RMS_EPS) * (GDN_DK ** -0.5)
    k = k * lax.rsqrt(jnp.sum(k * k, axis=-1, keepdims=True) + RMS_EPS)

    sm = sm_ref[...]
    lane = lax.broadcasted_iota(jnp.int32, sm.shape, 1)
    g_all = -jnp.exp(al_ref[...]) * _softplus(sm + dt_ref[...])
    g = jnp.sum(jnp.where(lane == SM_DA + h, g_all, 0.0), axis=1, keepdims=True)
    beta = jnp.sum(jnp.where(lane == SM_DB + h, _sigmoid(sm), 0.0), axis=1, keepdims=True)

    ri = lax.broadcasted_iota(jnp.int32, (C, C), 0)
    ci = lax.broadcasted_iota(jnp.int32, (C, C), 1)
    incl = ci <= ri
    strict = ci < ri
    eye = ci == ri
    nw = nw_ref[...]

    chunks = [slice(c * C, (c + 1) * C) for c in range(nc)]
    kb = [k[cs, :].astype(BF16) for cs in chunks]
    kk = [_dot_nt(x, x) for x in kb]
    qk_raw = [_dot_nt(q[cs, :].astype(BF16), x) for cs, x in zip(chunks, kb)]
    X, Pb, qk, q_dec, k_end, g_end = [], [], [], [], [], []
    for c, cs in enumerate(chunks):
        g_c, beta_c = g[cs, :], beta[cs, :]
        gam_row = jnp.sum(jnp.where(ri <= ci, g_c, 0.0), axis=0, keepdims=True)
        gam_col = jnp.sum(jnp.where(eye, gam_row, 0.0), axis=1, keepdims=True)
        gam_last = gam_col[C - 1:C, :]
        decay = jnp.where(incl, jnp.exp(jnp.where(incl, gam_col - gam_row, 0.0)), 0.0)
        e_gam = jnp.exp(gam_col)
        Pb.append(jnp.where(strict, beta_c * kk[c] * decay, 0.0).astype(BF16))
        X.append(jnp.concatenate([v[cs, :] * beta_c, k[cs, :] * (beta_c * e_gam)], axis=-1))
        qk.append(jnp.where(incl, qk_raw[c] * decay, 0.0).astype(BF16))
        q_dec.append((q[cs, :] * e_gam).astype(BF16))
        k_end.append((k[cs, :] * jnp.exp(gam_last - gam_col)).astype(BF16))
        g_end.append(jnp.exp(gam_last))

    def apply(P, x, sign):
        hi, lo = _split_bf16(x)
        return x + sign * (_dot(P, hi) + _dot(P, lo))

    X = [apply(P, x, -1.0) for P, x in zip(Pb, X)]
    for _ in range(5):
        Pb = [_dot(P, P).astype(BF16) for P in Pb]
        X = [apply(P, x, 1.0) for P, x in zip(Pb, X)]

    outs = []
    for c in range(nc):
        u, w = X[c][:, :GDN_DV], X[c][:, GDN_DV:]
        st = st_ref[...]
        sb = st.astype(BF16)
        v_new = u - _dot(w.astype(BF16), sb)
        vb = v_new.astype(BF16)
        outs.append(_dot(q_dec[c], sb) + _dot(qk[c], vb))
        st_ref[...] = g_end[c] * st + _dot_tn(k_end[c], vb)
    for cs, o in zip(chunks, outs):
        z_c = z_ref[cs, :]
        o = o * lax.rsqrt(jnp.mean(o * o, axis=-1, keepdims=True) + RMS_EPS) * nw
        o_ref[cs, :] = (o * (z_c * _sigmoid(z_c))).astype(o_ref.dtype)


def gdn_mixer(proj, small, conv_w, a_log, dt_bias, norm_w, *, batch, seq, ts=512):
    T = batch * seq
    ts = min(ts, seq)
    nt = seq // ts
    nc = ts // GDN_CHUNK
    H = GDN_HEADS

    def col(off):
        return pl.BlockSpec((ts, LANE), lambda b, h, t: (b * nt + t, off // LANE + h))

    def wcol(off):
        return pl.BlockSpec((GDN_CONV, LANE), lambda b, h, t: (0, off // LANE + h))

    def vec():
        return pl.BlockSpec((1, LANE), lambda b, h, t: (0, 0))

    a_row = jnp.zeros((1, SMALL_W), F32).at[0, SM_DA:SM_DA + H].set(a_log.astype(F32))
    dt_row = jnp.zeros((1, SMALL_W), F32).at[0, SM_DA:SM_DA + H].set(dt_bias.astype(F32))
    return pl.pallas_call(
        functools.partial(_gdn_kernel, nc=nc, ts=ts),
        grid=(batch, H, nt),
        in_specs=[col(OFF_DQ), col(OFF_DK), col(OFF_DV), col(OFF_DZ),
                  pl.BlockSpec((ts, SMALL_W), lambda b, h, t: (b * nt + t, 0)),
                  wcol(0), wcol(GDN_K_W), wcol(2 * GDN_K_W), vec(), vec(), vec()],
        out_specs=pl.BlockSpec((ts, LANE), lambda b, h, t: (b * nt + t, h)),
        out_shape=jax.ShapeDtypeStruct((T, GDN_V_W), BF16),
        scratch_shapes=[pltpu.VMEM((3, 8, LANE), F32), pltpu.VMEM((3, ts + 8, LANE), F32),
                        pltpu.VMEM((GDN_DK, GDN_DV), F32)],
        compiler_params=_cparams(("parallel", "parallel", "arbitrary")),
        name="gdn_mixer",
    )(proj, proj, proj, proj, small, conv_w, conv_w, conv_w, a_row, dt_row, norm_w.reshape(1, -1))


def _cross_kernel(q_ref, k_ref, v_ref, o_ref):
    scale = CROSS_DH ** -0.5
    for h in range(CROSS_HEADS):
        hs = slice(h * CROSS_DH, (h + 1) * CROSS_DH)
        s = _dot_nt(q_ref[:, hs], k_ref[:, hs]) * scale
        p = jnp.exp(s - jnp.max(s, axis=1, keepdims=True))
        o = _dot(p.astype(BF16), v_ref[:, hs]) / jnp.sum(p, axis=1, keepdims=True)
        o_ref[:, hs] = o.astype(o_ref.dtype)


def cross_attention(q, k, v, *, batch, seq, mem_len, tq=512):
    T, W = q.shape
    tq = min(tq, seq)
    nt = seq // tq
    kv = pl.BlockSpec((mem_len, W), lambda b, t: (b, 0))
    qo = pl.BlockSpec((tq, W), lambda b, t: (b * nt + t, 0))
    return pl.pallas_call(
        _cross_kernel,
        grid=(batch, nt),
        in_specs=[qo, kv, kv],
        out_specs=qo,
        out_shape=jax.ShapeDtypeStruct((T, W), BF16),
        compiler_params=_cparams(("parallel", "parallel")),
        name="cross_attn",
    )(q, k, v)


def kernel(x, mem, w_in, gla_w_g2, gla_b_g, gla_norm_w, gdn_conv_w, gdn_a_log, gdn_dt_bias, gdn_norm_w, w_branch, w_gate, b_gate, w_out, ln1_g, ln1_b, w_cq, w_ck, w_cv, w_co, ln2_g, ln2_b, w_up, w_down, ln3_g, ln3_b):
    B, S, D = x.shape
    M = mem.shape[1]
    depth = w_in.shape[0]
    alpha = (2.0 * depth) ** 0.25
    T = B * S
    h = x.reshape(T, D).astype(F32)
    hb = h.astype(BF16)
    mem_b = mem.reshape(B * M, D).astype(BF16)
    big = dict(tm=1024, tn=1024)
    for l in range(depth):
        w_main, w_small = split_w_in(w_in[l])
        proj = matmul(hb, w_main, **big)
        small = matmul(hb, w_small, **big)
        y_gla = gla_mixer(proj, small, gla_w_g2[l], gla_b_g[l], gla_norm_w[l], batch=B, seq=S)
        y_moba = moba_mixer(proj, batch=B, seq=S)
        y_gdn = gdn_mixer(proj, small, gdn_conv_w[l], gdn_a_log[l], gdn_dt_bias[l], gdn_norm_w[l],
                          batch=B, seq=S)
        gates = matmul(hb, cast_bf16(w_gate[l]), epi="bias_sigmoid", extra=b_gate[l].reshape(1, -1), **big)
        merged = gated_merge(y_gla, y_moba, y_gdn, cast_bf16(w_branch[l]), gates)
        pre = matmul(merged, cast_bf16(w_out[l]), epi="residual", extra=h, alpha=alpha, **big)
        h, hb = layer_norm(pre, ln1_g[l], ln1_b[l])

        cq = matmul(hb, cast_bf16(w_cq[l]), out_dtype=BF16, **big)
        ck = matmul(mem_b, cast_bf16(w_ck[l]), out_dtype=BF16, **big)
        cv = matmul(mem_b, cast_bf16(w_cv[l]), out_dtype=BF16, **big)
        att = cross_attention(cq, ck, cv, batch=B, seq=S, mem_len=M)
        pre = matmul(att, cast_bf16(w_co[l]), epi="residual", extra=h, alpha=alpha, **big)
        h, hb = layer_norm(pre, ln2_g[l], ln2_b[l])

        a = matmul(hb, cast_bf16(w_up[l]), epi="relu2", out_dtype=BF16, **big)
        pre = matmul(a, cast_bf16(w_down[l]), epi="residual", extra=h, alpha=alpha, tk=4096, **big)
        h, hb = layer_norm(pre, ln3_g[l], ln3_b[l])
    return h.reshape(B, S, D).astype(x.dtype)
```

```python
import functools

import jax
import jax.numpy as jnp
from jax import lax
from jax.experimental import pallas as pl
from jax.experimental.pallas import tpu as pltpu

F32 = jnp.float32
BF16 = jnp.bfloat16
HIGHEST = lax.Precision.HIGHEST

GLA_HEADS, GLA_DK, GLA_DV, GLA_RANK, GLA_TAU, GLA_CHUNK = 8, 64, 128, 16, 16.0, 64
MOBA_HEADS, MOBA_DH, MOBA_BLOCK, MOBA_TOPK = 16, 128, 256, 3
GDN_HEADS, GDN_DK, GDN_DV, GDN_CONV, GDN_CHUNK = 8, 128, 128, 4, 64
CROSS_HEADS, CROSS_DH = 4, 128
ROPE_THETA = 500000.0
ROT_DIM = MOBA_DH // 4
LN_EPS = 1e-5
RMS_EPS = 1e-6

GLA_QK_W = GLA_HEADS * GLA_DK
GLA_V_W = GLA_HEADS * GLA_DV
MOBA_W = MOBA_HEADS * MOBA_DH
GDN_K_W = GDN_HEADS * GDN_DK
GDN_V_W = GDN_HEADS * GDN_DV

OFF_MQ = 0
OFF_MK = OFF_MQ + MOBA_W
OFF_MV = OFF_MK + MOBA_W
OFF_GQ = OFF_MV + MOBA_W
OFF_GK = OFF_GQ + GLA_QK_W
OFF_GV = OFF_GK + GLA_QK_W
OFF_GR = OFF_GV + GLA_V_W
OFF_DQ = OFF_GR + GLA_V_W
OFF_DK = OFF_DQ + GDN_K_W
OFF_DV = OFF_DK + GDN_K_W
OFF_DZ = OFF_DV + GDN_V_W
MAIN_W = OFF_DZ + GDN_V_W
SMALL_W = 128
SM_GLOW = 0
SM_DA = GLA_RANK
SM_DB = SM_DA + GDN_HEADS

LANE = 128
MOBA_HEADS_PER_STEP = 4
MOBA_VT_ROWS = MOBA_DH + 8
LOG2_E = 1.4426950408889634
MOBA_BLOCKS_PER_TRIP = (2, 1)
VMEM_LIMIT = 56 * 1024 * 1024


def _cparams(sem):
    return pltpu.CompilerParams(dimension_semantics=sem, vmem_limit_bytes=VMEM_LIMIT)


def _dot(a, b, **kw):
    return jnp.dot(a, b, preferred_element_type=F32, **kw)


def _dot_nt(a, b, **kw):
    return lax.dot_general(a, b, (((1,), (1,)), ((), ())), preferred_element_type=F32, **kw)


def _dot_tn(a, b, **kw):
    return lax.dot_general(a, b, (((0,), (0,)), ((), ())), preferred_element_type=F32, **kw)


def _split_bf16(x):
    hi = x.astype(BF16)
    return hi, (x - hi.astype(F32)).astype(BF16)


def _sigmoid(x):
    return 1.0 / (1.0 + jnp.exp(-x))


def _softplus(x):
    return jnp.maximum(x, 0.0) + jnp.log1p(jnp.exp(-jnp.abs(x)))


def _log_sigmoid(x):
    return jnp.minimum(x, 0.0) - jnp.log1p(jnp.exp(-jnp.abs(x)))


def _mm_kernel(*refs, nk, epi, alpha):
    if epi in ("bias_sigmoid", "residual"):
        a_ref, w_ref, x_ref, o_ref = refs
    else:
        a_ref, w_ref, o_ref = refs
        x_ref = None

    def finish(acc):
        if epi == "bias_sigmoid":
            acc = _sigmoid(acc + x_ref[...])
        elif epi == "relu2":
            r = jnp.maximum(acc, 0.0)
            acc = r * r
        elif epi == "residual":
            acc = alpha * x_ref[...] + acc
        o_ref[...] = acc.astype(o_ref.dtype)

    part = _dot(a_ref[...], w_ref[...])
    if nk == 1:
        finish(part)
    else:
        k = pl.program_id(2)

        @pl.when(k == 0)
        def _():
            o_ref[...] = part

        @pl.when(jnp.logical_and(k > 0, k < nk - 1))
        def _():
            o_ref[...] += part

        @pl.when(k == nk - 1)
        def _():
            finish(o_ref[...] + part)


def matmul(a, w, *, tm, tn, tk=None, epi="none", extra=None, alpha=1.0, out_dtype=F32):
    M, K = a.shape
    _, N = w.shape
    tm, tn = min(tm, M), min(tn, N)
    tk = K if tk is None else min(tk, K)
    assert M % tm == 0 and N % tn == 0 and K % tk == 0
    nk = K // tk
    assert nk == 1 or out_dtype == F32
    in_specs = [pl.BlockSpec((tm, tk), lambda i, j, k: (i, k)),
                pl.BlockSpec((tk, tn), lambda i, j, k: (k, j))]
    args = [a, w]
    if epi == "bias_sigmoid":
        in_specs.append(pl.BlockSpec((1, tn), lambda i, j, k: (0, j)))
        args.append(extra)
    elif epi == "residual":
        in_specs.append(pl.BlockSpec((tm, tn), lambda i, j, k: (i, j)))
        args.append(extra)
    return pl.pallas_call(
        functools.partial(_mm_kernel, nk=nk, epi=epi, alpha=alpha),
        grid=(M // tm, N // tn, nk),
        in_specs=in_specs,
        out_specs=pl.BlockSpec((tm, tn), lambda i, j, k: (i, j)),
        out_shape=jax.ShapeDtypeStruct((M, N), out_dtype),
        compiler_params=_cparams(("parallel", "parallel", "arbitrary")),
        name="mm_" + epi,
    )(*args)


def _mm_ln_kernel(a_ref, w_ref, x_ref, g_ref, b_ref, of_ref, ob_ref, *, alpha):
    y = alpha * x_ref[...] + _dot(a_ref[...], w_ref[...])
    mu = jnp.mean(y, axis=-1, keepdims=True)
    yc = y - mu
    var = jnp.mean(yc * yc, axis=-1, keepdims=True)
    y = yc * lax.rsqrt(var + LN_EPS) * g_ref[...] + b_ref[...]
    of_ref[...] = y
    ob_ref[...] = y.astype(BF16)


def matmul_residual_ln(a, w, x, g, b, *, alpha, tm=256):
    M, K = a.shape
    _, N = w.shape
    tm = min(tm, M)
    row = pl.BlockSpec((tm, N), lambda i: (i, 0))
    vec = pl.BlockSpec((1, N), lambda i: (0, 0))
    return pl.pallas_call(
        functools.partial(_mm_ln_kernel, alpha=alpha),
        grid=(M // tm,),
        in_specs=[pl.BlockSpec((tm, K), lambda i: (i, 0)), pl.BlockSpec((K, N), lambda i: (0, 0)), row, vec, vec],
        out_specs=[row, row],
        out_shape=[jax.ShapeDtypeStruct((M, N), F32), jax.ShapeDtypeStruct((M, N), BF16)],
        compiler_params=_cparams(("parallel",)),
        name="mm_residual_ln",
    )(a, w, x, g.reshape(1, N), b.reshape(1, N))


CAST_BLOCK_BYTES = 8 * 1024 * 1024


def _cast_kernel(x_ref, o_ref):
    o_ref[...] = x_ref[...].astype(o_ref.dtype)


def cast_bf16(w, layer=None):
    R, N = w.shape[-2:]
    tr = 8
    while tr * 2 <= min(R, CAST_BLOCK_BYTES // (4 * N)):
        tr *= 2
    assert R % tr == 0
    out_spec = pl.BlockSpec((tr, N), lambda i: (i, 0))
    in_spec = out_spec if layer is None else pl.BlockSpec((None, tr, N), lambda i: (layer, i, 0))
    return pl.pallas_call(
        _cast_kernel, grid=(R // tr,), in_specs=[in_spec], out_specs=out_spec,
        out_shape=jax.ShapeDtypeStruct((R, N), BF16),
        compiler_params=_cparams(("parallel",)), name="cast_bf16",
    )(w)


def _split_w_in_kernel(x_ref, main_ref, small_ref):
    o_glow = 2 * GLA_QK_W + GLA_V_W
    o_gr = o_glow + GLA_RANK
    o_mq = o_gr + GLA_V_W
    o_dq = o_mq + 3 * MOBA_W
    o_da = o_dq + 2 * GDN_K_W + GDN_V_W
    o_dz = o_da + 2 * GDN_HEADS
    pos = 0
    for lo, hi in ((o_mq, o_dq), (0, o_glow), (o_gr, o_mq), (o_dq, o_da), (o_dz, o_dz + GDN_V_W)):
        main_ref[:, pos:pos + hi - lo] = x_ref[:, lo:hi].astype(BF16)
        pos += hi - lo
    small_ref[...] = jnp.zeros_like(small_ref)
    small_ref[:, SM_GLOW:SM_GLOW + GLA_RANK] = x_ref[:, o_glow:o_gr].astype(BF16)
    small_ref[:, SM_DA:SM_DA + 2 * GDN_HEADS] = x_ref[:, o_da:o_dz].astype(BF16)


def split_w_in(w, layer):
    _, R, N = w.shape
    tr = 128
    return pl.pallas_call(
        _split_w_in_kernel, grid=(R // tr,),
        in_specs=[pl.BlockSpec((None, tr, N), lambda i: (layer, i, 0))],
        out_specs=[pl.BlockSpec((tr, MAIN_W), lambda i: (i, 0)), pl.BlockSpec((tr, SMALL_W), lambda i: (i, 0))],
        out_shape=[jax.ShapeDtypeStruct((R, MAIN_W), BF16), jax.ShapeDtypeStruct((R, SMALL_W), BF16)],
        compiler_params=_cparams(("parallel",)), name="split_w_in",
    )(w)


def _ln_kernel(x_ref, g_ref, b_ref, of_ref, ob_ref):
    x = x_ref[...]
    mu = jnp.mean(x, axis=-1, keepdims=True)
    xc = x - mu
    var = jnp.mean(xc * xc, axis=-1, keepdims=True)
    y = xc * lax.rsqrt(var + LN_EPS) * g_ref[...] + b_ref[...]
    of_ref[...] = y
    ob_ref[...] = y.astype(BF16)


def layer_norm(x, g, b, *, tm=256):
    M, D = x.shape
    tm = min(tm, M)
    row = pl.BlockSpec((tm, D), lambda i: (i, 0))
    vec = pl.BlockSpec((1, D), lambda i: (0, 0))
    return pl.pallas_call(
        _ln_kernel,
        grid=(M // tm,),
        in_specs=[row, vec, vec],
        out_specs=[row, row],
        out_shape=[jax.ShapeDtypeStruct((M, D), F32), jax.ShapeDtypeStruct((M, D), BF16)],
        compiler_params=_cparams(("parallel",)),
        name="layer_norm",
    )(x, g.reshape(1, D), b.reshape(1, D))


def _merge_kernel(ya_ref, yb_ref, yc_ref, w_ref, g0_ref, g1_ref, g2_ref, o_ref):
    a0, a1 = GLA_V_W, GLA_V_W + MOBA_W
    acc = g0_ref[...] * _dot(ya_ref[...], w_ref[0:a0, :])
    acc += g1_ref[...] * _dot(yb_ref[...], w_ref[a0:a1, :])
    acc += g2_ref[...] * _dot(yc_ref[...], w_ref[a1:, :])
    o_ref[...] = acc.astype(o_ref.dtype)


def gated_merge(y_gla, y_moba, y_gdn, w_branch, gates, *, tm=1024, tn=512):
    M = y_gla.shape[0]
    KW, D = w_branch.shape
    tm, tn = min(tm, M), min(tn, D)
    nj = D // tn

    def yspec(width):
        return pl.BlockSpec((tm, width), lambda i, j: (i, 0))

    def gspec(branch):
        return pl.BlockSpec((tm, tn), lambda i, j: (i, branch * nj + j))

    return pl.pallas_call(
        _merge_kernel,
        grid=(M // tm, nj),
        in_specs=[yspec(GLA_V_W), yspec(MOBA_W), yspec(GDN_V_W),
                  pl.BlockSpec((KW, tn), lambda i, j: (0, j)),
                  gspec(0), gspec(1), gspec(2)],
        out_specs=pl.BlockSpec((tm, tn), lambda i, j: (i, j)),
        out_shape=jax.ShapeDtypeStruct((M, D), BF16),
        compiler_params=_cparams(("parallel", "parallel")),
        name="gated_merge",
    )(y_gla, y_moba, y_gdn, w_branch, gates, gates, gates)


def _gla_kernel(q_ref, k_ref, v_ref, r_ref, sm_ref, wg2_ref, bg_ref, nw_ref, o_ref, st_ref, *, nc):
    C = GLA_CHUNK

    @pl.when(pl.program_id(1) == 0)
    def _():
        st_ref[...] = jnp.zeros_like(st_ref)

    glow = sm_ref[:, SM_GLOW:SM_GLOW + GLA_RANK]
    z = _dot(glow, wg2_ref[...], precision=HIGHEST) + bg_ref[...]
    log_a = _log_sigmoid(z) * (1.0 / GLA_TAU)
    ri = lax.broadcasted_iota(jnp.int32, (C, C), 0)
    ci = lax.broadcasted_iota(jnp.int32, (C, C), 1)
    causal = ci <= ri
    tri = jnp.where(causal, 1.0, 0.0).astype(F32)
    nw = nw_ref[...]
    scale = GLA_DK ** -0.5

    for c in range(nc):
        cs = slice(c * C, (c + 1) * C)
        G = _dot(tri, log_a[cs, :], precision=HIGHEST)
        g_mid = G[C // 2 - 1:C // 2, :]
        g_last = G[C - 1:C, :]
        qc = q_ref[cs, :] * scale
        kc = k_ref[cs, :]
        q_in = (qc * jnp.exp(G - g_mid)).astype(BF16)
        k_in = (kc * jnp.exp(g_mid - G)).astype(BF16)
        k_end = (kc * jnp.exp(g_last - G)).astype(BF16)
        q_dec = (qc * jnp.exp(G)).astype(BF16)
        dec = jnp.exp(g_last)
        heads = [(h, slice(h * GLA_DK, (h + 1) * GLA_DK), slice(h * GLA_DV, (h + 1) * GLA_DV))
                 for h in range(GLA_HEADS)]
        A = [_dot_nt(q_in[:, hs], k_in[:, hs]) for _, hs, _ in heads]
        v_b = [v_ref[cs, vs].astype(BF16) for _, _, vs in heads]
        st = [st_ref[h] for h, _, _ in heads]
        o_inter = [_dot_nt(q_dec[:, hs], st[h].astype(BF16)) for h, hs, _ in heads]
        o_intra = [_dot(jnp.where(causal, A[h], 0.0).astype(BF16), v_b[h]) for h, _, _ in heads]
        upd = [_dot_tn(v_b[h], k_end[:, hs]) for h, hs, _ in heads]
        for h, hs, vs in heads:
            st_ref[h] = st[h] * dec[:, hs] + upd[h]
            o = o_intra[h] + o_inter[h]
            r_h = r_ref[cs, vs]
            o = o * lax.rsqrt(jnp.mean(o * o, axis=-1, keepdims=True) + RMS_EPS) * nw
            o_ref[cs, vs] = (o * (r_h * _sigmoid(r_h))).astype(o_ref.dtype)


def gla_mixer(proj, small, w_g2, b_g, norm_w, *, batch, seq, ts=256):
    T = batch * seq
    ts = min(ts, seq)
    nt = seq // ts
    nc = ts // GLA_CHUNK

    def col(width, off):
        return pl.BlockSpec((ts, width), lambda b, t: (b * nt + t, off // width))

    def full(shape):
        return pl.BlockSpec(shape, lambda b, t: (0,) * len(shape))

    return pl.pallas_call(
        functools.partial(_gla_kernel, nc=nc),
        grid=(batch, nt),
        in_specs=[col(GLA_QK_W, OFF_GQ), col(GLA_QK_W, OFF_GK), col(GLA_V_W, OFF_GV), col(GLA_V_W, OFF_GR),
                  pl.BlockSpec((ts, SMALL_W), lambda b, t: (b * nt + t, 0)),
                  full((GLA_RANK, GLA_QK_W)), full((1, GLA_QK_W)), full((1, GLA_DV))],
        out_specs=pl.BlockSpec((ts, GLA_V_W), lambda b, t: (b * nt + t, 0)),
        out_shape=jax.ShapeDtypeStruct((T, GLA_V_W), BF16),
        scratch_shapes=[pltpu.VMEM((GLA_HEADS, GLA_DV, GLA_DK), F32)],
        compiler_params=_cparams(("parallel", "arbitrary")),
        name="gla_mixer",
    )(proj, proj, proj, proj, small, w_g2, b_g.reshape(1, -1), norm_w.reshape(1, -1))


def _rope_tables(seq):
    half = ROT_DIM // 2
    inv = jnp.power(ROPE_THETA, -jnp.arange(half, dtype=F32) * 2.0 / ROT_DIM)
    ang = jnp.arange(seq, dtype=jnp.int32).astype(F32)[:, None] * inv[None, :]
    cos, sin = jnp.cos(ang), jnp.sin(ang)
    pad = jnp.zeros((seq, MOBA_DH - ROT_DIM), F32)
    zero = jnp.zeros((seq, half), F32)
    t_cos = jnp.concatenate([cos, cos, pad + 1.0], axis=-1)
    t_lo = jnp.concatenate([-sin, zero, pad], axis=-1)
    t_hi = jnp.concatenate([zero, sin, pad], axis=-1)
    return t_cos, t_lo, t_hi


def _moba_prep_kernel(q_ref, k_ref, v_ref, tc_ref, tl_ref, th_ref, qo_ref, ko_ref, vt_ref, km_ref):
    half = ROT_DIM // 2
    tc, tl, th = tc_ref[...], tl_ref[...], th_ref[...]

    def rot(x):
        return x * tc + pltpu.roll(x, MOBA_DH - half, axis=1) * tl + pltpu.roll(x, half, axis=1) * th

    for h in range(MOBA_HEADS):
        hs = slice(h * MOBA_DH, (h + 1) * MOBA_DH)
        qo_ref[:, hs] = rot(q_ref[:, hs])
        kr = rot(k_ref[:, hs])
        ko_ref[:, hs] = kr.astype(BF16)
        km_ref[0, :, hs] = jnp.mean(kr, axis=0, keepdims=True)
        vt_ref[0, h, 0:MOBA_DH, :] = v_ref[:, hs].T.astype(BF16)
        vt_ref[0, h, MOBA_DH:, :] = jnp.ones((MOBA_VT_ROWS - MOBA_DH, MOBA_BLOCK), BF16)


def _moba_attn_kernel(q_ref, k_ref, vt_ref, km_ref, o_ref, qs_ref, acc_ref, sel_ref, *, topk, heads):
    i = pl.program_id(2)
    Bk = MOBA_BLOCK
    nb = km_ref.shape[0]
    hslices = [slice(g * MOBA_DH, (g + 1) * MOBA_DH) for g in range(heads)]
    key = lax.broadcasted_iota(jnp.int32, (Bk, Bk), 0)
    qry = lax.broadcasted_iota(jnp.int32, (Bk, Bk), 1)
    blk = lax.broadcasted_iota(jnp.int32, (nb, Bk), 0)
    own = pl.ds(pl.multiple_of(i * Bk, Bk), Bk)

    for g, hs in enumerate(hslices):
        qs_ref[g] = (q_ref[:, hs] * (MOBA_DH ** -0.5 * LOG2_E)).astype(BF16)
    scores = [_dot_nt(k_ref[own, hs], qs_ref[g]) for g, hs in enumerate(hslices)]
    gates = [_dot_nt(km_ref[:, hs], q_ref[:, hs], precision=HIGHEST) for hs in hslices]
    for g in range(heads):
        gate = jnp.where(blk < i, gates[g], -jnp.inf)
        sel = jnp.zeros(gate.shape, F32)
        for r in range(topk):
            top = jnp.max(gate, axis=0, keepdims=True)
            idx = jnp.min(jnp.where(gate == top, blk, nb), axis=0, keepdims=True)
            hit = blk == idx
            sel = jnp.where(jnp.logical_and(hit, i > r), 1.0, sel)
            gate = jnp.where(hit, -jnp.inf, gate)
        sel_ref[g] = sel
    init, probs = [], []
    for g in range(heads):
        s = jnp.where(key <= qry, scores[g], -jnp.inf)
        m0 = jnp.max(s, axis=0, keepdims=True)
        init.append(m0)
        probs.append(jnp.exp2(s - m0).astype(BF16))
    for g in range(heads):
        acc_ref[g] = _dot(vt_ref[i, g], probs[g])

    def visit(blocks, carry):
        sc = [[_dot_nt(k_ref[pl.ds(pl.multiple_of(j * Bk, Bk), Bk), hs], qs_ref[g]) for g, hs in enumerate(hslices)]
              for j in blocks]
        m_run = list(carry)
        pr, scale = [], []
        for n, j in enumerate(blocks):
            pr.append([])
            scale.append([])
            for g in range(heads):
                picked = sel_ref[g, pl.ds(j, 1), :] > 0.5
                m_blk = jnp.where(picked, jnp.max(sc[n][g], axis=0, keepdims=True), -jnp.inf)
                m_new = jnp.maximum(m_run[g], m_blk)
                pr[n].append(jnp.exp2(sc[n][g] - jnp.where(picked, m_new, jnp.inf)).astype(BF16))
                scale[n].append(jnp.exp2(m_run[g] - m_new))
                m_run[g] = m_new
        pv = [[_dot(vt_ref[j, g], pr[n][g]) for g in range(heads)] for n, j in enumerate(blocks)]
        for g in range(heads):
            acc = acc_ref[g]
            for n in range(len(blocks)):
                acc = scale[n][g] * acc + pv[n][g]
            acc_ref[g] = acc
        return tuple(m_run)

    carry, done = tuple(init), 0
    for width in MOBA_BLOCKS_PER_TRIP:
        trips = (i - done) // width
        carry = lax.fori_loop(0, trips, lambda t, c, w=width, d=done: visit([d + t * w + n for n in range(w)], c), carry)
        done = done + trips * width
    for g, hs in enumerate(hslices):
        acc = acc_ref[g]
        o_ref[:, hs] = (acc[0:MOBA_DH, :] / acc[MOBA_DH:MOBA_DH + 1, :]).T.astype(o_ref.dtype)


def moba_mixer(proj, *, batch, seq):
    T = batch * seq
    Bk = MOBA_BLOCK
    assert seq % Bk == 0
    nb = seq // Bk
    topk = max(1, min(MOBA_TOPK, nb - 1))
    t_cos, t_lo, t_hi = _rope_tables(seq)

    def col(off):
        return pl.BlockSpec((Bk, MOBA_W), lambda t: (t, off // MOBA_W))

    tab = pl.BlockSpec((Bk, MOBA_DH), lambda t: (t % nb, 0))
    wide = pl.BlockSpec((Bk, MOBA_W), lambda t: (t, 0))
    q_rot, k_rot, v_t, k_mean = pl.pallas_call(
        _moba_prep_kernel,
        grid=(T // Bk,),
        in_specs=[col(OFF_MQ), col(OFF_MK), col(OFF_MV), tab, tab, tab],
        out_specs=[wide, wide, pl.BlockSpec((1, MOBA_HEADS, MOBA_VT_ROWS, Bk), lambda t: (t, 0, 0, 0)),
                   pl.BlockSpec((1, 1, MOBA_W), lambda t: (t, 0, 0))],
        out_shape=[jax.ShapeDtypeStruct((T, MOBA_W), F32), jax.ShapeDtypeStruct((T, MOBA_W), BF16),
                   jax.ShapeDtypeStruct((T // Bk, MOBA_HEADS, MOBA_VT_ROWS, Bk), BF16),
                   jax.ShapeDtypeStruct((T // Bk, 1, MOBA_W), F32)],
        compiler_params=_cparams(("parallel",)),
        name="moba_prep",
    )(proj, proj, proj, t_cos, t_lo, t_hi)
    k_mean = k_mean.reshape(batch, nb, MOBA_W)
    v_t = v_t.reshape(batch, nb, MOBA_HEADS, MOBA_VT_ROWS, Bk)

    G = MOBA_HEADS_PER_STEP
    GW = G * MOBA_DH
    qo = pl.BlockSpec((Bk, GW), lambda b, h, i: (b * nb + i, h))
    return pl.pallas_call(
        functools.partial(_moba_attn_kernel, topk=topk, heads=G),
        grid=(batch, MOBA_HEADS // G, nb),
        in_specs=[qo, pl.BlockSpec((seq, GW), lambda b, h, i: (b, h)),
                  pl.BlockSpec((None, nb, G, MOBA_VT_ROWS, Bk), lambda b, h, i: (b, 0, h, 0, 0)),
                  pl.BlockSpec((None, nb, GW), lambda b, h, i: (b, 0, h))],
        out_specs=qo,
        out_shape=jax.ShapeDtypeStruct((T, MOBA_W), BF16),
        scratch_shapes=[pltpu.VMEM((G, Bk, MOBA_DH), BF16), pltpu.VMEM((G, MOBA_VT_ROWS, Bk), F32),
                        pltpu.VMEM((G, nb, Bk), F32)],
        compiler_params=_cparams(("parallel", "parallel", "arbitrary")),
        name="moba_attn",
    )(q_rot, k_rot, v_t, k_mean)


def _gdn_kernel(q_ref, k_ref, v_ref, z_ref, sm_ref, wq_ref, wk_ref, wv_ref, al_ref, dt_ref, nw_ref,
                o_ref, tail_ref, xbuf_ref, st_ref, *, nc, ts):
    C = GDN_CHUNK
    PADR = 8
    h = pl.program_id(1)

    @pl.when(pl.program_id(2) == 0)
    def _():
        tail_ref[...] = jnp.zeros_like(tail_ref)
        st_ref[...] = jnp.zeros_like(st_ref)

    def conv_silu(n, x_ref, w_ref):
        xbuf_ref[n, 0:PADR, :] = tail_ref[n]
        xbuf_ref[n, PADR:PADR + ts, :] = x_ref[...]
        acc = None
        for j in range(GDN_CONV):
            lo = PADR - (GDN_CONV - 1) + j
            term = w_ref[j:j + 1, :] * xbuf_ref[n, lo:lo + ts, :]
            acc = term if acc is None else acc + term
        tail_ref[n] = x_ref[ts - PADR:ts, :]
        return acc * _sigmoid(acc)

    q = conv_silu(0, q_ref, wq_ref)
    k = conv_silu(1, k_ref, wk_ref)
    v = conv_silu(2, v_ref, wv_ref)
    q = q * lax.rsqrt(jnp.sum(q * q, axis=-1, keepdims=True) + RMS_EPS) * (GDN_DK ** -0.5)
    k = k * lax.rsqrt(jnp.sum(k * k, axis=-1, keepdims=True) + RMS_EPS)

    sm = sm_ref[...]
    lane = lax.broadcasted_iota(jnp.int32, sm.shape, 1)
    g_all = -jnp.exp(al_ref[...]) * _softplus(sm + dt_ref[...])
    g = jnp.sum(jnp.where(lane == SM_DA + h, g_all, 0.0), axis=1, keepdims=True)
    beta = jnp.sum(jnp.where(lane == SM_DB + h, _sigmoid(sm), 0.0), axis=1, keepdims=True)

    ri = lax.broadcasted_iota(jnp.int32, (C, C), 0)
    ci = lax.broadcasted_iota(jnp.int32, (C, C), 1)
    incl = ci <= ri
    strict = ci < ri
    eye = ci == ri
    nw = nw_ref[...]

    chunks = [slice(c * C, (c + 1) * C) for c in range(nc)]
    kb = [k[cs, :].astype(BF16) for cs in chunks]
    kk = [_dot_nt(x, x) for x in kb]
    qk_raw = [_dot_nt(q[cs, :].astype(BF16), x) for cs, x in zip(chunks, kb)]
    X, Pb, qk, q_dec, k_end, g_end = [], [], [], [], [], []
    for c, cs in enumerate(chunks):
        g_c, beta_c = g[cs, :], beta[cs, :]
        gam_row = jnp.sum(jnp.where(ri <= ci, g_c, 0.0), axis=0, keepdims=True)
        gam_col = jnp.sum(jnp.where(eye, gam_row, 0.0), axis=1, keepdims=True)
        gam_last = gam_col[C - 1:C, :]
        decay = jnp.where(incl, jnp.exp(jnp.where(incl, gam_col - gam_row, 0.0)), 0.0)
        e_gam = jnp.exp(gam_col)
        Pb.append(jnp.where(strict, beta_c * kk[c] * decay, 0.0).astype(BF16))
        X.append(jnp.concatenate([v[cs, :] * beta_c, k[cs, :] * (beta_c * e_gam)], axis=-1))
        qk.append(jnp.where(incl, qk_raw[c] * decay, 0.0).astype(BF16))
        q_dec.append((q[cs, :] * e_gam).astype(BF16))
        k_end.append((k[cs, :] * jnp.exp(gam_last - gam_col)).astype(BF16))
        g_end.append(jnp.exp(gam_last))

    def apply(P, x, sign):
        hi, lo = _split_bf16(x)
        return x + sign * (_dot(P, hi) + _dot(P, lo))

    X = [apply(P, x, -1.0) for P, x in zip(Pb, X)]
    for _ in range(5):
        Pb = [_dot(P, P).astype(BF16) for P in Pb]
        X = [apply(P, x, 1.0) for P, x in zip(Pb, X)]

    outs = []
    for c in range(nc):
        u, w = X[c][:, :GDN_DV], X[c][:, GDN_DV:]
        st = st_ref[...]
        sb = st.astype(BF16)
        v_new = u - _dot(w.astype(BF16), sb)
        vb = v_new.astype(BF16)
        outs.append(_dot(q_dec[c], sb) + _dot(qk[c], vb))
        st_ref[...] = g_end[c] * st + _dot_tn(k_end[c], vb)
    for cs, o in zip(chunks, outs):
        z_c = z_ref[cs, :]
        o = o * lax.rsqrt(jnp.mean(o * o, axis=-1, keepdims=True) + RMS_EPS) * nw
        o_ref[cs, :] = (o * (z_c * _sigmoid(z_c))).astype(o_ref.dtype)


def gdn_mixer(proj, small, conv_w, a_log, dt_bias, norm_w, *, batch, seq, ts=512):
    T = batch * seq
    ts = min(ts, seq)
    nt = seq // ts
    nc = ts // GDN_CHUNK
    H = GDN_HEADS

    def col(off):
        return pl.BlockSpec((ts, LANE), lambda b, h, t: (b * nt + t, off // LANE + h))

    def wcol(off):
        return pl.BlockSpec((GDN_CONV, LANE), lambda b, h, t: (0, off // LANE + h))

    def vec():
        return pl.BlockSpec((1, LANE), lambda b, h, t: (0, 0))

    a_row = jnp.zeros((1, SMALL_W), F32).at[0, SM_DA:SM_DA + H].set(a_log.astype(F32))
    dt_row = jnp.zeros((1, SMALL_W), F32).at[0, SM_DA:SM_DA + H].set(dt_bias.astype(F32))
    return pl.pallas_call(
        functools.partial(_gdn_kernel, nc=nc, ts=ts),
        grid=(batch, H, nt),
        in_specs=[col(OFF_DQ), col(OFF_DK), col(OFF_DV), col(OFF_DZ),
                  pl.BlockSpec((ts, SMALL_W), lambda b, h, t: (b * nt + t, 0)),
                  wcol(0), wcol(GDN_K_W), wcol(2 * GDN_K_W), vec(), vec(), vec()],
        out_specs=pl.BlockSpec((ts, LANE), lambda b, h, t: (b * nt + t, h)),
        out_shape=jax.ShapeDtypeStruct((T, GDN_V_W), BF16),
        scratch_shapes=[pltpu.VMEM((3, 8, LANE), F32), pltpu.VMEM((3, ts + 8, LANE), F32),
                        pltpu.VMEM((GDN_DK, GDN_DV), F32)],
        compiler_params=_cparams(("parallel", "parallel", "arbitrary")),
        name="gdn_mixer",
    )(proj, proj, proj, proj, small, conv_w, conv_w, conv_w, a_row, dt_row, norm_w.reshape(1, -1))


def _cross_kernel(q_ref, k_ref, v_ref, o_ref):
    scale = CROSS_DH ** -0.5
    for h in range(CROSS_HEADS):
        hs = slice(h * CROSS_DH, (h + 1) * CROSS_DH)
        s = _dot_nt(q_ref[:, hs], k_ref[:, hs]) * scale
        p = jnp.exp(s - jnp.max(s, axis=1, keepdims=True))
        o = _dot(p.astype(BF16), v_ref[:, hs]) / jnp.sum(p, axis=1, keepdims=True)
        o_ref[:, hs] = o.astype(o_ref.dtype)


def cross_attention(q, k, v, *, batch, seq, mem_len, tq=512):
    T, W = q.shape
    tq = min(tq, seq)
    nt = seq // tq
    kv = pl.BlockSpec((mem_len, W), lambda b, t: (b, 0))
    qo = pl.BlockSpec((tq, W), lambda b, t: (b * nt + t, 0))
    return pl.pallas_call(
        _cross_kernel,
        grid=(batch, nt),
        in_specs=[qo, kv, kv],
        out_specs=qo,
        out_shape=jax.ShapeDtypeStruct((T, W), BF16),
        compiler_params=_cparams(("parallel", "parallel")),
        name="cross_attn",
    )(q, k, v)


def kernel(x, mem, w_in, gla_w_g2, gla_b_g, gla_norm_w, gdn_conv_w, gdn_a_log, gdn_dt_bias, gdn_norm_w, w_branch, w_gate, b_gate, w_out, ln1_g, ln1_b, w_cq, w_ck, w_cv, w_co, ln2_g, ln2_b, w_up, w_down, ln3_g, ln3_b):
    B, S, D = x.shape
    M = mem.shape[1]
    depth = w_in.shape[0]
    alpha = (2.0 * depth) ** 0.25
    T = B * S
    h = x.reshape(T, D).astype(F32)
    hb = cast_bf16(h)
    mem_b = cast_bf16(mem.reshape(B * M, D))
    big = dict(tm=1024, tn=1024)
    for l in range(depth):
        w_main, w_small = split_w_in(w_in, l)
        proj = matmul(hb, w_main, **big)
        small = matmul(hb, w_small, **big)
        y_gla = gla_mixer(proj, small, gla_w_g2[l], gla_b_g[l], gla_norm_w[l], batch=B, seq=S)
        y_moba = moba_mixer(proj, batch=B, seq=S)
        y_gdn = gdn_mixer(proj, small, gdn_conv_w[l], gdn_a_log[l], gdn_dt_bias[l], gdn_norm_w[l],
                          batch=B, seq=S)
        gates = matmul(hb, cast_bf16(w_gate, l), epi="bias_sigmoid", extra=b_gate[l].reshape(1, -1), **big)
        merged = gated_merge(y_gla, y_moba, y_gdn, cast_bf16(w_branch, l), gates)
        pre = matmul(merged, cast_bf16(w_out, l), epi="residual", extra=h, alpha=alpha, **big)
        h, hb = layer_norm(pre, ln1_g[l], ln1_b[l])

        cq = matmul(hb, cast_bf16(w_cq, l), out_dtype=BF16, **big)
        ck = matmul(mem_b, cast_bf16(w_ck, l), out_dtype=BF16, **big)
        cv = matmul(mem_b, cast_bf16(w_cv, l), out_dtype=BF16, **big)
        att = cross_attention(cq, ck, cv, batch=B, seq=S, mem_len=M)
        h, hb = matmul_residual_ln(att, cast_bf16(w_co, l), h, ln2_g[l], ln2_b[l], alpha=alpha)

        a = matmul(hb, cast_bf16(w_up, l), epi="relu2", out_dtype=BF16, **big)
        pre = matmul(a, cast_bf16(w_down, l), epi="residual", extra=h, alpha=alpha, tk=2048, **big)
        h, hb = layer_norm(pre, ln3_g[l], ln3_b[l])
    return h.reshape(B, S, D).astype(x.dtype)
```

```python
import functools

import jax
import jax.numpy as jnp
from jax import lax
from jax.experimental import pallas as pl
from jax.experimental.pallas import tpu as pltpu

F32 = jnp.float32
BF16 = jnp.bfloat16
HIGHEST = lax.Precision.HIGHEST

GLA_HEADS, GLA_DK, GLA_DV, GLA_RANK, GLA_TAU, GLA_CHUNK = 8, 64, 128, 16, 16.0, 64
MOBA_HEADS, MOBA_DH, MOBA_BLOCK, MOBA_TOPK = 16, 128, 256, 3
GDN_HEADS, GDN_DK, GDN_DV, GDN_CONV, GDN_CHUNK = 8, 128, 128, 4, 64
CROSS_HEADS, CROSS_DH = 4, 128
ROPE_THETA = 500000.0
ROT_DIM = MOBA_DH // 4
LN_EPS = 1e-5
RMS_EPS = 1e-6

GLA_QK_W = GLA_HEADS * GLA_DK
GLA_V_W = GLA_HEADS * GLA_DV
MOBA_W = MOBA_HEADS * MOBA_DH
GDN_K_W = GDN_HEADS * GDN_DK
GDN_V_W = GDN_HEADS * GDN_DV

OFF_MQ = 0
OFF_MK = OFF_MQ + MOBA_W
OFF_MV = OFF_MK + MOBA_W
OFF_GQ = OFF_MV + MOBA_W
OFF_GK = OFF_GQ + GLA_QK_W
OFF_GV = OFF_GK + GLA_QK_W
OFF_GR = OFF_GV + GLA_V_W
OFF_DQ = OFF_GR + GLA_V_W
OFF_DK = OFF_DQ + GDN_K_W
OFF_DV = OFF_DK + GDN_K_W
OFF_DZ = OFF_DV + GDN_V_W
MAIN_W = OFF_DZ + GDN_V_W
SMALL_W = 128
SM_GLOW = 0
SM_DA = GLA_RANK
SM_DB = SM_DA + GDN_HEADS

LANE = 128
MOBA_HEADS_PER_STEP = 4
MOBA_VT_ROWS = MOBA_DH + 8
LOG2_E = 1.4426950408889634
MOBA_BLOCKS_PER_TRIP = (4, 2, 1)
VMEM_LIMIT = 56 * 1024 * 1024


def _cparams(sem):
    return pltpu.CompilerParams(dimension_semantics=sem, vmem_limit_bytes=VMEM_LIMIT)


def _dot(a, b, **kw):
    return jnp.dot(a, b, preferred_element_type=F32, **kw)


def _dot_nt(a, b, **kw):
    return lax.dot_general(a, b, (((1,), (1,)), ((), ())), preferred_element_type=F32, **kw)


def _dot_tn(a, b, **kw):
    return lax.dot_general(a, b, (((0,), (0,)), ((), ())), preferred_element_type=F32, **kw)


def _split_bf16(x):
    hi = x.astype(BF16)
    return hi, (x - hi.astype(F32)).astype(BF16)


def _sigmoid(x):
    return 1.0 / (1.0 + jnp.exp(-x))


def _softplus(x):
    return jnp.maximum(x, 0.0) + jnp.log1p(jnp.exp(-jnp.abs(x)))


def _log_sigmoid(x):
    return jnp.minimum(x, 0.0) - jnp.log1p(jnp.exp(-jnp.abs(x)))


def _mm_kernel(*refs, nk, epi, alpha):
    if epi in ("bias_sigmoid", "residual"):
        a_ref, w_ref, x_ref, o_ref = refs
    else:
        a_ref, w_ref, o_ref = refs
        x_ref = None

    def finish(acc):
        if epi == "bias_sigmoid":
            acc = _sigmoid(acc + x_ref[...])
        elif epi == "relu2":
            r = jnp.maximum(acc, 0.0)
            acc = r * r
        elif epi == "residual":
            acc = alpha * x_ref[...] + acc
        o_ref[...] = acc.astype(o_ref.dtype)

    part = _dot(a_ref[...], w_ref[...])
    if nk == 1:
        finish(part)
    else:
        k = pl.program_id(2)

        @pl.when(k == 0)
        def _():
            o_ref[...] = part

        @pl.when(jnp.logical_and(k > 0, k < nk - 1))
        def _():
            o_ref[...] += part

        @pl.when(k == nk - 1)
        def _():
            finish(o_ref[...] + part)


def matmul(a, w, *, tm, tn, tk=None, epi="none", extra=None, alpha=1.0, out_dtype=F32):
    M, K = a.shape
    _, N = w.shape
    tm, tn = min(tm, M), min(tn, N)
    tk = K if tk is None else min(tk, K)
    assert M % tm == 0 and N % tn == 0 and K % tk == 0
    nk = K // tk
    assert nk == 1 or out_dtype == F32
    in_specs = [pl.BlockSpec((tm, tk), lambda i, j, k: (i, k)),
                pl.BlockSpec((tk, tn), lambda i, j, k: (k, j))]
    args = [a, w]
    if epi == "bias_sigmoid":
        in_specs.append(pl.BlockSpec((1, tn), lambda i, j, k: (0, j)))
        args.append(extra)
    elif epi == "residual":
        in_specs.append(pl.BlockSpec((tm, tn), lambda i, j, k: (i, j)))
        args.append(extra)
    return pl.pallas_call(
        functools.partial(_mm_kernel, nk=nk, epi=epi, alpha=alpha),
        grid=(M // tm, N // tn, nk),
        in_specs=in_specs,
        out_specs=pl.BlockSpec((tm, tn), lambda i, j, k: (i, j)),
        out_shape=jax.ShapeDtypeStruct((M, N), out_dtype),
        compiler_params=_cparams(("parallel", "parallel", "arbitrary")),
        name="mm_" + epi,
    )(*args)


def _mm_ln_kernel(a_ref, w_ref, x_ref, g_ref, b_ref, of_ref, ob_ref, *, alpha):
    y = alpha * x_ref[...] + _dot(a_ref[...], w_ref[...])
    mu = jnp.mean(y, axis=-1, keepdims=True)
    yc = y - mu
    var = jnp.mean(yc * yc, axis=-1, keepdims=True)
    y = yc * lax.rsqrt(var + LN_EPS) * g_ref[...] + b_ref[...]
    of_ref[...] = y
    ob_ref[...] = y.astype(BF16)


def matmul_residual_ln(a, w, x, g, b, *, alpha, tm=256):
    M, K = a.shape
    _, N = w.shape
    tm = min(tm, M)
    row = pl.BlockSpec((tm, N), lambda i: (i, 0))
    vec = pl.BlockSpec((1, N), lambda i: (0, 0))
    return pl.pallas_call(
        functools.partial(_mm_ln_kernel, alpha=alpha),
        grid=(M // tm,),
        in_specs=[pl.BlockSpec((tm, K), lambda i: (i, 0)), pl.BlockSpec((K, N), lambda i: (0, 0)), row, vec, vec],
        out_specs=[row, row],
        out_shape=[jax.ShapeDtypeStruct((M, N), F32), jax.ShapeDtypeStruct((M, N), BF16)],
        compiler_params=_cparams(("parallel",)),
        name="mm_residual_ln",
    )(a, w, x, g.reshape(1, N), b.reshape(1, N))


CAST_BLOCK_BYTES = 8 * 1024 * 1024


def _cast_kernel(x_ref, o_ref):
    o_ref[...] = x_ref[...].astype(o_ref.dtype)


def cast_bf16(w, layer=None):
    R, N = w.shape[-2:]
    tr = 8
    while tr * 2 <= min(R, CAST_BLOCK_BYTES // (4 * N)):
        tr *= 2
    assert R % tr == 0
    out_spec = pl.BlockSpec((tr, N), lambda i: (i, 0))
    in_spec = out_spec if layer is None else pl.BlockSpec((None, tr, N), lambda i: (layer, i, 0))
    return pl.pallas_call(
        _cast_kernel, grid=(R // tr,), in_specs=[in_spec], out_specs=out_spec,
        out_shape=jax.ShapeDtypeStruct((R, N), BF16),
        compiler_params=_cparams(("parallel",)), name="cast_bf16",
    )(w)


W_IN_SPLIT_LANES = 256


def _split_w_in_kernel(x_ref, main_ref, small_ref):
    o_glow = 2 * GLA_QK_W + GLA_V_W
    o_gr = o_glow + GLA_RANK
    o_mq = o_gr + GLA_V_W
    o_dq = o_mq + 3 * MOBA_W
    o_da = o_dq + 2 * GDN_K_W + GDN_V_W
    o_dz = o_da + 2 * GDN_HEADS
    pos = 0
    for lo, hi in ((o_mq, o_dq), (0, o_glow), (o_gr, o_mq), (o_dq, o_da), (o_dz, o_dz + GDN_V_W)):
        main_ref[pos:pos + hi - lo, :] = x_ref[lo:hi, :].astype(BF16)
        pos += hi - lo
    small_ref[...] = jnp.zeros_like(small_ref)
    small_ref[SM_GLOW:SM_GLOW + GLA_RANK, :] = x_ref[o_glow:o_gr, :].astype(BF16)
    small_ref[SM_DA:SM_DA + 2 * GDN_HEADS, :] = x_ref[o_da:o_dz, :].astype(BF16)


def split_w_in(w_t, layer):
    _, N, D = w_t.shape
    tc = W_IN_SPLIT_LANES
    return pl.pallas_call(
        _split_w_in_kernel, grid=(D // tc,),
        in_specs=[pl.BlockSpec((None, N, tc), lambda i: (layer, 0, i))],
        out_specs=[pl.BlockSpec((MAIN_W, tc), lambda i: (0, i)), pl.BlockSpec((SMALL_W, tc), lambda i: (0, i))],
        out_shape=[jax.ShapeDtypeStruct((MAIN_W, D), BF16), jax.ShapeDtypeStruct((SMALL_W, D), BF16)],
        compiler_params=_cparams(("parallel",)), name="split_w_in",
    )(w_t)


def _mm_nt_kernel(a_ref, wt_ref, o_ref):
    o_ref[...] = _dot_nt(a_ref[...], wt_ref[...]).astype(o_ref.dtype)


def matmul_nt(a, w_t, *, tm, tn, out_dtype=F32):
    M, K = a.shape
    N, _ = w_t.shape
    tm, tn = min(tm, M), min(tn, N)
    assert M % tm == 0 and N % tn == 0
    return pl.pallas_call(
        _mm_nt_kernel,
        grid=(M // tm, N // tn),
        in_specs=[pl.BlockSpec((tm, K), lambda i, j: (i, 0)), pl.BlockSpec((tn, K), lambda i, j: (j, 0))],
        out_specs=pl.BlockSpec((tm, tn), lambda i, j: (i, j)),
        out_shape=jax.ShapeDtypeStruct((M, N), out_dtype),
        compiler_params=_cparams(("parallel", "parallel")),
        name="mm_nt",
    )(a, w_t)


def _ln_kernel(x_ref, g_ref, b_ref, of_ref, ob_ref):
    x = x_ref[...]
    mu = jnp.mean(x, axis=-1, keepdims=True)
    xc = x - mu
    var = jnp.mean(xc * xc, axis=-1, keepdims=True)
    y = xc * lax.rsqrt(var + LN_EPS) * g_ref[...] + b_ref[...]
    of_ref[...] = y
    ob_ref[...] = y.astype(BF16)


def layer_norm(x, g, b, *, tm=256):
    M, D = x.shape
    tm = min(tm, M)
    row = pl.BlockSpec((tm, D), lambda i: (i, 0))
    vec = pl.BlockSpec((1, D), lambda i: (0, 0))
    return pl.pallas_call(
        _ln_kernel,
        grid=(M // tm,),
        in_specs=[row, vec, vec],
        out_specs=[row, row],
        out_shape=[jax.ShapeDtypeStruct((M, D), F32), jax.ShapeDtypeStruct((M, D), BF16)],
        compiler_params=_cparams(("parallel",)),
        name="layer_norm",
    )(x, g.reshape(1, D), b.reshape(1, D))


def _merge_kernel(ya_ref, yb_ref, yc_ref, w_ref, g0_ref, g1_ref, g2_ref, o_ref):
    a0, a1 = GLA_V_W, GLA_V_W + MOBA_W
    acc = g0_ref[...] * _dot(ya_ref[...], w_ref[0:a0, :])
    acc += g1_ref[...] * _dot(yb_ref[...], w_ref[a0:a1, :])
    acc += g2_ref[...] * _dot(yc_ref[...], w_ref[a1:, :])
    o_ref[...] = acc.astype(o_ref.dtype)


def gated_merge(y_gla, y_moba, y_gdn, w_branch, gates, *, tm=1024, tn=512):
    M = y_gla.shape[0]
    KW, D = w_branch.shape
    tm, tn = min(tm, M), min(tn, D)
    nj = D // tn

    def yspec(width):
        return pl.BlockSpec((tm, width), lambda i, j: (i, 0))

    def gspec(branch):
        return pl.BlockSpec((tm, tn), lambda i, j: (i, branch * nj + j))

    return pl.pallas_call(
        _merge_kernel,
        grid=(M // tm, nj),
        in_specs=[yspec(GLA_V_W), yspec(MOBA_W), yspec(GDN_V_W),
                  pl.BlockSpec((KW, tn), lambda i, j: (0, j)),
                  gspec(0), gspec(1), gspec(2)],
        out_specs=pl.BlockSpec((tm, tn), lambda i, j: (i, j)),
        out_shape=jax.ShapeDtypeStruct((M, D), BF16),
        compiler_params=_cparams(("parallel", "parallel")),
        name="gated_merge",
    )(y_gla, y_moba, y_gdn, w_branch, gates, gates, gates)


def _gla_kernel(q_ref, k_ref, v_ref, r_ref, sm_ref, wg2_ref, bg_ref, nw_ref, o_ref, st_ref, *, nc):
    C = GLA_CHUNK

    @pl.when(pl.program_id(1) == 0)
    def _():
        st_ref[...] = jnp.zeros_like(st_ref)

    glow = sm_ref[:, SM_GLOW:SM_GLOW + GLA_RANK]
    z = _dot(glow, wg2_ref[...], precision=HIGHEST) + bg_ref[...]
    log_a = _log_sigmoid(z) * (1.0 / GLA_TAU)
    ri = lax.broadcasted_iota(jnp.int32, (C, C), 0)
    ci = lax.broadcasted_iota(jnp.int32, (C, C), 1)
    causal = ci <= ri
    tri = jnp.where(causal, 1.0, 0.0).astype(F32)
    nw = nw_ref[...]
    scale = GLA_DK ** -0.5

    for c in range(nc):
        cs = slice(c * C, (c + 1) * C)
        G = _dot(tri, log_a[cs, :], precision=HIGHEST)
        g_mid = G[C // 2 - 1:C // 2, :]
        g_last = G[C - 1:C, :]
        qc = q_ref[cs, :] * scale
        kc = k_ref[cs, :]
        q_in = (qc * jnp.exp(G - g_mid)).astype(BF16)
        k_in = (kc * jnp.exp(g_mid - G)).astype(BF16)
        k_end = (kc * jnp.exp(g_last - G)).astype(BF16)
        q_dec = (qc * jnp.exp(G)).astype(BF16)
        dec = jnp.exp(g_last)
        heads = [(h, slice(h * GLA_DK, (h + 1) * GLA_DK), slice(h * GLA_DV, (h + 1) * GLA_DV))
                 for h in range(GLA_HEADS)]
        A = [_dot_nt(q_in[:, hs], k_in[:, hs]) for _, hs, _ in heads]
        v_b = [v_ref[cs, vs].astype(BF16) for _, _, vs in heads]
        st = [st_ref[h] for h, _, _ in heads]
        o_inter = [_dot_nt(q_dec[:, hs], st[h].astype(BF16)) for h, hs, _ in heads]
        o_intra = [_dot(jnp.where(causal, A[h], 0.0).astype(BF16), v_b[h]) for h, _, _ in heads]
        upd = [_dot_tn(v_b[h], k_end[:, hs]) for h, hs, _ in heads]
        for h, hs, vs in heads:
            st_ref[h] = st[h] * dec[:, hs] + upd[h]
            o = o_intra[h] + o_inter[h]
            r_h = r_ref[cs, vs]
            o = o * lax.rsqrt(jnp.mean(o * o, axis=-1, keepdims=True) + RMS_EPS) * nw
            o_ref[cs, vs] = (o * (r_h * _sigmoid(r_h))).astype(o_ref.dtype)


def gla_mixer(proj, small, w_g2, b_g, norm_w, *, batch, seq, ts=256):
    T = batch * seq
    ts = min(ts, seq)
    nt = seq // ts
    nc = ts // GLA_CHUNK

    def col(width, off):
        return pl.BlockSpec((ts, width), lambda b, t: (b * nt + t, off // width))

    def full(shape):
        return pl.BlockSpec(shape, lambda b, t: (0,) * len(shape))

    return pl.pallas_call(
        functools.partial(_gla_kernel, nc=nc),
        grid=(batch, nt),
        in_specs=[col(GLA_QK_W, OFF_GQ), col(GLA_QK_W, OFF_GK), col(GLA_V_W, OFF_GV), col(GLA_V_W, OFF_GR),
                  pl.BlockSpec((ts, SMALL_W), lambda b, t: (b * nt + t, 0)),
                  full((GLA_RANK, GLA_QK_W)), full((1, GLA_QK_W)), full((1, GLA_DV))],
        out_specs=pl.BlockSpec((ts, GLA_V_W), lambda b, t: (b * nt + t, 0)),
        out_shape=jax.ShapeDtypeStruct((T, GLA_V_W), BF16),
        scratch_shapes=[pltpu.VMEM((GLA_HEADS, GLA_DV, GLA_DK), F32)],
        compiler_params=_cparams(("parallel", "arbitrary")),
        name="gla_mixer",
    )(proj, proj, proj, proj, small, w_g2, b_g.reshape(1, -1), norm_w.reshape(1, -1))


def _rope_tables(seq):
    half = ROT_DIM // 2
    inv = jnp.power(ROPE_THETA, -jnp.arange(half, dtype=F32) * 2.0 / ROT_DIM)
    ang = jnp.arange(seq, dtype=jnp.int32).astype(F32)[:, None] * inv[None, :]
    cos, sin = jnp.cos(ang), jnp.sin(ang)
    pad = jnp.zeros((seq, MOBA_DH - ROT_DIM), F32)
    zero = jnp.zeros((seq, half), F32)
    t_cos = jnp.concatenate([cos, cos, pad + 1.0], axis=-1)
    t_lo = jnp.concatenate([-sin, zero, pad], axis=-1)
    t_hi = jnp.concatenate([zero, sin, pad], axis=-1)
    return t_cos, t_lo, t_hi


def _moba_prep_kernel(q_ref, k_ref, v_ref, tc_ref, tl_ref, th_ref, qo_ref, ko_ref, vt_ref, km_ref):
    half = ROT_DIM // 2
    tc, tl, th = tc_ref[...], tl_ref[...], th_ref[...]

    def rot(x):
        return x * tc + pltpu.roll(x, MOBA_DH - half, axis=1) * tl + pltpu.roll(x, half, axis=1) * th

    for h in range(MOBA_HEADS):
        hs = slice(h * MOBA_DH, (h + 1) * MOBA_DH)
        qo_ref[:, hs] = rot(q_ref[:, hs])
        kr = rot(k_ref[:, hs])
        ko_ref[:, hs] = kr.astype(BF16)
        km_ref[0, :, hs] = jnp.mean(kr, axis=0, keepdims=True)
        vt_ref[0, h, 0:MOBA_DH, :] = v_ref[:, hs].T.astype(BF16)
        vt_ref[0, h, MOBA_DH:, :] = jnp.ones((MOBA_VT_ROWS - MOBA_DH, MOBA_BLOCK), BF16)


def _moba_attn_kernel(q_ref, k_ref, vt_ref, km_ref, o_ref, qs_ref, acc_ref, sel_ref, *, topk, heads):
    i = pl.program_id(2)
    Bk = MOBA_BLOCK
    nb = km_ref.shape[0]
    hslices = [slice(g * MOBA_DH, (g + 1) * MOBA_DH) for g in range(heads)]
    key = lax.broadcasted_iota(jnp.int32, (Bk, Bk), 0)
    qry = lax.broadcasted_iota(jnp.int32, (Bk, Bk), 1)
    blk = lax.broadcasted_iota(jnp.int32, (nb, Bk), 0)
    own = pl.ds(pl.multiple_of(i * Bk, Bk), Bk)

    for g, hs in enumerate(hslices):
        qs_ref[g] = (q_ref[:, hs] * (MOBA_DH ** -0.5 * LOG2_E)).astype(BF16)
    scores = [_dot_nt(k_ref[own, hs], qs_ref[g]) for g, hs in enumerate(hslices)]
    gates = [_dot_nt(km_ref[:, hs], q_ref[:, hs], precision=HIGHEST) for hs in hslices]
    for g in range(heads):
        gate = jnp.where(blk < i, gates[g], -jnp.inf)
        sel = jnp.zeros(gate.shape, F32)
        for r in range(topk):
            top = jnp.max(gate, axis=0, keepdims=True)
            idx = jnp.min(jnp.where(gate == top, blk, nb), axis=0, keepdims=True)
            hit = blk == idx
            sel = jnp.where(jnp.logical_and(hit, i > r), 1.0, sel)
            gate = jnp.where(hit, -jnp.inf, gate)
        sel_ref[g] = sel
    init, probs = [], []
    for g in range(heads):
        s = jnp.where(key <= qry, scores[g], -jnp.inf)
        m0 = jnp.max(s, axis=0, keepdims=True)
        init.append(m0)
        probs.append(jnp.exp2(s - m0).astype(BF16))
    for g in range(heads):
        acc_ref[g] = _dot(vt_ref[i, g], probs[g])

    def visit(blocks, carry):
        sc = [[_dot_nt(k_ref[pl.ds(pl.multiple_of(j * Bk, Bk), Bk), hs], qs_ref[g]) for g, hs in enumerate(hslices)]
              for j in blocks]
        m_run = list(carry)
        pr, scale = [], []
        for n, j in enumerate(blocks):
            pr.append([])
            scale.append([])
            for g in range(heads):
                picked = sel_ref[g, pl.ds(j, 1), :] > 0.5
                m_blk = jnp.where(picked, jnp.max(sc[n][g], axis=0, keepdims=True), -jnp.inf)
                m_new = jnp.maximum(m_run[g], m_blk)
                pr[n].append(jnp.exp2(sc[n][g] - jnp.where(picked, m_new, jnp.inf)).astype(BF16))
                scale[n].append(jnp.exp2(m_run[g] - m_new))
                m_run[g] = m_new
        pv = [[_dot(vt_ref[j, g], pr[n][g]) for g in range(heads)] for n, j in enumerate(blocks)]
        for g in range(heads):
            acc = acc_ref[g]
            for n in range(len(blocks)):
                acc = scale[n][g] * acc + pv[n][g]
            acc_ref[g] = acc
        return tuple(m_run)

    carry, done = tuple(init), 0
    for width in MOBA_BLOCKS_PER_TRIP:
        trips = (i - done) // width
        carry = lax.fori_loop(0, trips, lambda t, c, w=width, d=done: visit([d + t * w + n for n in range(w)], c), carry)
        done = done + trips * width
    for g, hs in enumerate(hslices):
        acc = acc_ref[g]
        o_ref[:, hs] = (acc[0:MOBA_DH, :] / acc[MOBA_DH:MOBA_DH + 1, :]).T.astype(o_ref.dtype)


def moba_mixer(proj, *, batch, seq):
    T = batch * seq
    Bk = MOBA_BLOCK
    assert seq % Bk == 0
    nb = seq // Bk
    topk = max(1, min(MOBA_TOPK, nb - 1))
    t_cos, t_lo, t_hi = _rope_tables(seq)

    def col(off):
        return pl.BlockSpec((Bk, MOBA_W), lambda t: (t, off // MOBA_W))

    tab = pl.BlockSpec((Bk, MOBA_DH), lambda t: (t % nb, 0))
    wide = pl.BlockSpec((Bk, MOBA_W), lambda t: (t, 0))
    q_rot, k_rot, v_t, k_mean = pl.pallas_call(
        _moba_prep_kernel,
        grid=(T // Bk,),
        in_specs=[col(OFF_MQ), col(OFF_MK), col(OFF_MV), tab, tab, tab],
        out_specs=[wide, wide, pl.BlockSpec((1, MOBA_HEADS, MOBA_VT_ROWS, Bk), lambda t: (t, 0, 0, 0)),
                   pl.BlockSpec((1, 1, MOBA_W), lambda t: (t, 0, 0))],
        out_shape=[jax.ShapeDtypeStruct((T, MOBA_W), F32), jax.ShapeDtypeStruct((T, MOBA_W), BF16),
                   jax.ShapeDtypeStruct((T // Bk, MOBA_HEADS, MOBA_VT_ROWS, Bk), BF16),
                   jax.ShapeDtypeStruct((T // Bk, 1, MOBA_W), F32)],
        compiler_params=_cparams(("parallel",)),
        name="moba_prep",
    )(proj, proj, proj, t_cos, t_lo, t_hi)
    k_mean = k_mean.reshape(batch, nb, MOBA_W)
    v_t = v_t.reshape(batch, nb, MOBA_HEADS, MOBA_VT_ROWS, Bk)

    G = MOBA_HEADS_PER_STEP
    GW = G * MOBA_DH
    qo = pl.BlockSpec((Bk, GW), lambda b, h, i: (b * nb + i, h))
    return pl.pallas_call(
        functools.partial(_moba_attn_kernel, topk=topk, heads=G),
        grid=(batch, MOBA_HEADS // G, nb),
        in_specs=[qo, pl.BlockSpec((seq, GW), lambda b, h, i: (b, h)),
                  pl.BlockSpec((None, nb, G, MOBA_VT_ROWS, Bk), lambda b, h, i: (b, 0, h, 0, 0)),
                  pl.BlockSpec((None, nb, GW), lambda b, h, i: (b, 0, h))],
        out_specs=qo,
        out_shape=jax.ShapeDtypeStruct((T, MOBA_W), BF16),
        scratch_shapes=[pltpu.VMEM((G, Bk, MOBA_DH), BF16), pltpu.VMEM((G, MOBA_VT_ROWS, Bk), F32),
                        pltpu.VMEM((G, nb, Bk), F32)],
        compiler_params=_cparams(("parallel", "parallel", "arbitrary")),
        name="moba_attn",
    )(q_rot, k_rot, v_t, k_mean)


def _gdn_kernel(q_ref, k_ref, v_ref, z_ref, sm_ref, wq_ref, wk_ref, wv_ref, al_ref, dt_ref, nw_ref,
                o_ref, tail_ref, xbuf_ref, st_ref, *, nc, ts):
    C = GDN_CHUNK
    PADR = 8
    h = pl.program_id(1)

    @pl.when(pl.program_id(2) == 0)
    def _():
        tail_ref[...] = jnp.zeros_like(tail_ref)
        st_ref[...] = jnp.zeros_like(st_ref)

    def conv_silu(n, x_ref, w_ref):
        xbuf_ref[n, 0:PADR, :] = tail_ref[n]
        xbuf_ref[n, PADR:PADR + ts, :] = x_ref[...]
        acc = None
        for j in range(GDN_CONV):
            lo = PADR - (GDN_CONV - 1) + j
            term = w_ref[j:j + 1, :] * xbuf_ref[n, lo:lo + ts, :]
            acc = term if acc is None else acc + term
        tail_ref[n] = x_ref[ts - PADR:ts, :]
        return acc * _sigmoid(acc)

    q = conv_silu(0, q_ref, wq_ref)
    k = conv_silu(1, k_ref, wk_ref)
    v = conv_silu(2, v_ref, wv_ref)
    q = q * lax.rsqrt(jnp.sum(q * q, axis=-1, keepdims=True) + RMS_EPS) * (GDN_DK ** -0.5)
    k = k * lax.rsqrt(jnp.sum(k * k, axis=-1, keepdims=True) + RMS_EPS)

    sm = sm_ref[...]
    lane = lax.broadcasted_iota(jnp.int32, sm.shape, 1)
    g_all = -jnp.exp(al_ref[...]) * _softplus(sm + dt_ref[...])
    g = jnp.sum(jnp.where(lane == SM_DA + h, g_all, 0.0), axis=1, keepdims=True)
    beta = jnp.sum(jnp.where(lane == SM_DB + h, _sigmoid(sm), 0.0), axis=1, keepdims=True)

    ri = lax.broadcasted_iota(jnp.int32, (C, C), 0)
    ci = lax.broadcasted_iota(jnp.int32, (C, C), 1)
    incl = ci <= ri
    strict = ci < ri
    eye = ci == ri
    nw = nw_ref[...]

    chunks = [slice(c * C, (c + 1) * C) for c in range(nc)]
    kb = [k[cs, :].astype(BF16) for cs in chunks]
    kk = [_dot_nt(x, x) for x in kb]
    qk_raw = [_dot_nt(q[cs, :].astype(BF16), x) for cs, x in zip(chunks, kb)]
    X, Pb, qk, q_dec, k_end, g_end = [], [], [], [], [], []
    for c, cs in enumerate(chunks):
        g_c, beta_c = g[cs, :], beta[cs, :]
        gam_row = jnp.sum(jnp.where(ri <= ci, g_c, 0.0), axis=0, keepdims=True)
        gam_col = jnp.sum(jnp.where(eye, gam_row, 0.0), axis=1, keepdims=True)
        gam_last = gam_col[C - 1:C, :]
        decay = jnp.where(incl, jnp.exp(jnp.where(incl, gam_col - gam_row, 0.0)), 0.0)
        e_gam = jnp.exp(gam_col)
        Pb.append(jnp.where(strict, beta_c * kk[c] * decay, 0.0).astype(BF16))
        X.append(jnp.concatenate([v[cs, :] * beta_c, k[cs, :] * (beta_c * e_gam)], axis=-1))
        qk.append(jnp.where(incl, qk_raw[c] * decay, 0.0).astype(BF16))
        q_dec.append(q[cs, :] * e_gam)
        k_end.append((k[cs, :] * jnp.exp(gam_last - gam_col)).astype(BF16))
        g_end.append(jnp.exp(gam_last))

    def apply(P, x, sign):
        hi, lo = _split_bf16(x)
        return x + sign * (_dot(P, hi) + _dot(P, lo))

    X = [apply(P, x, -1.0) for P, x in zip(Pb, X)]
    for _ in range(5):
        Pb = [_dot(P, P).astype(BF16) for P in Pb]
        X = [apply(P, x, 1.0) for P, x in zip(Pb, X)]

    ub = [x[:, :GDN_DV].astype(BF16) for x in X]
    wb = [x[:, GDN_DV:].astype(BF16) for x in X]
    s_op = [(-_dot_tn(k_end[c], wb[c])).astype(BF16) for c in range(nc)]
    s_add = [_dot_tn(k_end[c], ub[c]) for c in range(nc)]
    o_op = [(q_dec[c] - _dot(qk[c], wb[c])).astype(BF16) for c in range(nc)]
    o_add = [_dot(qk[c], ub[c]) for c in range(nc)]
    outs = []
    for c in range(nc):
        st = st_ref[...]
        sb = st.astype(BF16)
        outs.append(_dot(o_op[c], sb) + o_add[c])
        st_ref[...] = g_end[c] * st + (_dot(s_op[c], sb) + s_add[c])
    for cs, o in zip(chunks, outs):
        z_c = z_ref[cs, :]
        o = o * lax.rsqrt(jnp.mean(o * o, axis=-1, keepdims=True) + RMS_EPS) * nw
        o_ref[cs, :] = (o * (z_c * _sigmoid(z_c))).astype(o_ref.dtype)


def gdn_mixer(proj, small, conv_w, a_log, dt_bias, norm_w, *, batch, seq, ts=512):
    T = batch * seq
    ts = min(ts, seq)
    nt = seq // ts
    nc = ts // GDN_CHUNK
    H = GDN_HEADS

    def col(off):
        return pl.BlockSpec((ts, LANE), lambda b, h, t: (b * nt + t, off // LANE + h))

    def wcol(off):
        return pl.BlockSpec((GDN_CONV, LANE), lambda b, h, t: (0, off // LANE + h))

    def vec():
        return pl.BlockSpec((1, LANE), lambda b, h, t: (0, 0))

    a_row = jnp.zeros((1, SMALL_W), F32).at[0, SM_DA:SM_DA + H].set(a_log.astype(F32))
    dt_row = jnp.zeros((1, SMALL_W), F32).at[0, SM_DA:SM_DA + H].set(dt_bias.astype(F32))
    return pl.pallas_call(
        functools.partial(_gdn_kernel, nc=nc, ts=ts),
        grid=(batch, H, nt),
        in_specs=[col(OFF_DQ), col(OFF_DK), col(OFF_DV), col(OFF_DZ),
                  pl.BlockSpec((ts, SMALL_W), lambda b, h, t: (b * nt + t, 0)),
                  wcol(0), wcol(GDN_K_W), wcol(2 * GDN_K_W), vec(), vec(), vec()],
        out_specs=pl.BlockSpec((ts, LANE), lambda b, h, t: (b * nt + t, h)),
        out_shape=jax.ShapeDtypeStruct((T, GDN_V_W), BF16),
        scratch_shapes=[pltpu.VMEM((3, 8, LANE), F32), pltpu.VMEM((3, ts + 8, LANE), F32),
                        pltpu.VMEM((GDN_DK, GDN_DV), F32)],
        compiler_params=_cparams(("parallel", "parallel", "arbitrary")),
        name="gdn_mixer",
    )(proj, proj, proj, proj, small, conv_w, conv_w, conv_w, a_row, dt_row, norm_w.reshape(1, -1))


def _cross_kernel(q_ref, k_ref, v_ref, o_ref):
    scale = CROSS_DH ** -0.5
    for h in range(CROSS_HEADS):
        hs = slice(h * CROSS_DH, (h + 1) * CROSS_DH)
        s = _dot_nt(q_ref[:, hs], k_ref[:, hs]) * scale
        p = jnp.exp(s - jnp.max(s, axis=1, keepdims=True))
        o = _dot(p.astype(BF16), v_ref[:, hs]) / jnp.sum(p, axis=1, keepdims=True)
        o_ref[:, hs] = o.astype(o_ref.dtype)


def cross_attention(q, k, v, *, batch, seq, mem_len, tq=512):
    T, W = q.shape
    tq = min(tq, seq)
    nt = seq // tq
    kv = pl.BlockSpec((mem_len, W), lambda b, t: (b, 0))
    qo = pl.BlockSpec((tq, W), lambda b, t: (b * nt + t, 0))
    return pl.pallas_call(
        _cross_kernel,
        grid=(batch, nt),
        in_specs=[qo, kv, kv],
        out_specs=qo,
        out_shape=jax.ShapeDtypeStruct((T, W), BF16),
        compiler_params=_cparams(("parallel", "parallel")),
        name="cross_attn",
    )(q, k, v)


def kernel(x, mem, w_in, gla_w_g2, gla_b_g, gla_norm_w, gdn_conv_w, gdn_a_log, gdn_dt_bias, gdn_norm_w, w_branch, w_gate, b_gate, w_out, ln1_g, ln1_b, w_cq, w_ck, w_cv, w_co, ln2_g, ln2_b, w_up, w_down, ln3_g, ln3_b):
    B, S, D = x.shape
    M = mem.shape[1]
    depth = w_in.shape[0]
    alpha = (2.0 * depth) ** 0.25
    T = B * S
    h = x.reshape(T, D).astype(F32)
    hb = cast_bf16(h)
    mem_b = cast_bf16(mem.reshape(B * M, D))
    big = dict(tm=1024, tn=1024)
    w_in_t = jnp.swapaxes(w_in, 1, 2)
    for l in range(depth):
        w_main_t, w_small_t = split_w_in(w_in_t, l)
        proj = matmul_nt(hb, w_main_t, **big)
        small = matmul_nt(hb, w_small_t, **big)
        y_gla = gla_mixer(proj, small, gla_w_g2[l], gla_b_g[l], gla_norm_w[l], batch=B, seq=S)
        y_moba = moba_mixer(proj, batch=B, seq=S)
        y_gdn = gdn_mixer(proj, small, gdn_conv_w[l], gdn_a_log[l], gdn_dt_bias[l], gdn_norm_w[l],
                          batch=B, seq=S)
        gates = matmul(hb, cast_bf16(w_gate, l), epi="bias_sigmoid", extra=b_gate[l].reshape(1, -1), **big)
        merged = gated_merge(y_gla, y_moba, y_gdn, cast_bf16(w_branch, l), gates)
        pre = matmul(merged, cast_bf16(w_out, l), epi="residual", extra=h, alpha=alpha, **big)
        h, hb = layer_norm(pre, ln1_g[l], ln1_b[l])

        cq = matmul(hb, cast_bf16(w_cq, l), out_dtype=BF16, **big)
        ck = matmul(mem_b, cast_bf16(w_ck, l), out_dtype=BF16, **big)
        cv = matmul(mem_b, cast_bf16(w_cv, l), out_dtype=BF16, **big)
        att = cross_attention(cq, ck, cv, batch=B, seq=S, mem_len=M)
        h, hb = matmul_residual_ln(att, cast_bf16(w_co, l), h, ln2_g[l], ln2_b[l], alpha=alpha)

        a = matmul(hb, cast_bf16(w_up, l), epi="relu2", out_dtype=BF16, **big)
        pre = matmul(a, cast_bf16(w_down, l), epi="residual", extra=h, alpha=alpha, tk=2048, **big)
        h, hb = layer_norm(pre, ln3_g[l], ln3_b[l])
    return h.reshape(B, S, D).astype(x.dtype)
```

```python
import functools

import jax
import jax.numpy as jnp
from jax import lax
from jax.experimental import pallas as pl
from jax.experimental.pallas import tpu as pltpu

F32 = jnp.float32
BF16 = jnp.bfloat16
HIGHEST = lax.Precision.HIGHEST

GLA_HEADS, GLA_DK, GLA_DV, GLA_RANK, GLA_TAU, GLA_CHUNK = 8, 64, 128, 16, 16.0, 64
MOBA_HEADS, MOBA_DH, MOBA_BLOCK, MOBA_TOPK = 16, 128, 256, 3
GDN_HEADS, GDN_DK, GDN_DV, GDN_CONV, GDN_CHUNK = 8, 128, 128, 4, 64
CROSS_HEADS, CROSS_DH = 4, 128
ROPE_THETA = 500000.0
ROT_DIM = MOBA_DH // 4
LN_EPS = 1e-5
RMS_EPS = 1e-6

GLA_QK_W = GLA_HEADS * GLA_DK
GLA_V_W = GLA_HEADS * GLA_DV
MOBA_W = MOBA_HEADS * MOBA_DH
GDN_K_W = GDN_HEADS * GDN_DK
GDN_V_W = GDN_HEADS * GDN_DV

OFF_MQ = 0
OFF_MK = OFF_MQ + MOBA_W
OFF_MV = OFF_MK + MOBA_W
OFF_GQ = OFF_MV + MOBA_W
OFF_GK = OFF_GQ + GLA_QK_W
OFF_GV = OFF_GK + GLA_QK_W
OFF_GR = OFF_GV + GLA_V_W
OFF_DQ = OFF_GR + GLA_V_W
OFF_DK = OFF_DQ + GDN_K_W
OFF_DV = OFF_DK + GDN_K_W
OFF_DZ = OFF_DV + GDN_V_W
MAIN_W = OFF_DZ + GDN_V_W
SMALL_W = 128
SM_GLOW = 0
SM_DA = GLA_RANK
SM_DB = SM_DA + GDN_HEADS

LANE = 128
MOBA_HEADS_PER_STEP = 8
MOBA_VT_ROWS = MOBA_DH + 8
LOG2_E = 1.4426950408889634
MOBA_BLOCKS_PER_TRIP = (4, 2, 1)
VMEM_LIMIT = 56 * 1024 * 1024


def _cparams(sem):
    return pltpu.CompilerParams(dimension_semantics=sem, vmem_limit_bytes=VMEM_LIMIT)


def _dot(a, b, **kw):
    return jnp.dot(a, b, preferred_element_type=F32, **kw)


def _dot_nt(a, b, **kw):
    return lax.dot_general(a, b, (((1,), (1,)), ((), ())), preferred_element_type=F32, **kw)


def _dot_tn(a, b, **kw):
    return lax.dot_general(a, b, (((0,), (0,)), ((), ())), preferred_element_type=F32, **kw)


def _split_bf16(x):
    hi = x.astype(BF16)
    return hi, (x - hi.astype(F32)).astype(BF16)


def _sigmoid(x):
    return 1.0 / (1.0 + jnp.exp(-x))


def _softplus(x):
    return jnp.maximum(x, 0.0) + jnp.log1p(jnp.exp(-jnp.abs(x)))


def _log_sigmoid(x):
    return jnp.minimum(x, 0.0) - jnp.log1p(jnp.exp(-jnp.abs(x)))


def _mm_kernel(*refs, nk, epi, alpha):
    if epi in ("bias_sigmoid", "residual"):
        a_ref, w_ref, x_ref, o_ref = refs
    else:
        a_ref, w_ref, o_ref = refs
        x_ref = None

    def finish(acc):
        if epi == "bias_sigmoid":
            acc = _sigmoid(acc + x_ref[...])
        elif epi == "relu2":
            r = jnp.maximum(acc, 0.0)
            acc = r * r
        elif epi == "residual":
            acc = alpha * x_ref[...] + acc
        o_ref[...] = acc.astype(o_ref.dtype)

    part = _dot(a_ref[...], w_ref[...])
    if nk == 1:
        finish(part)
    else:
        k = pl.program_id(2)

        @pl.when(k == 0)
        def _():
            o_ref[...] = part

        @pl.when(jnp.logical_and(k > 0, k < nk - 1))
        def _():
            o_ref[...] += part

        @pl.when(k == nk - 1)
        def _():
            finish(o_ref[...] + part)


def matmul(a, w, *, tm, tn, tk=None, epi="none", extra=None, alpha=1.0, out_dtype=F32):
    M, K = a.shape
    _, N = w.shape
    tm, tn = min(tm, M), min(tn, N)
    tk = K if tk is None else min(tk, K)
    assert M % tm == 0 and N % tn == 0 and K % tk == 0
    nk = K // tk
    assert nk == 1 or out_dtype == F32
    in_specs = [pl.BlockSpec((tm, tk), lambda i, j, k: (i, k)),
                pl.BlockSpec((tk, tn), lambda i, j, k: (k, j))]
    args = [a, w]
    if epi == "bias_sigmoid":
        in_specs.append(pl.BlockSpec((1, tn), lambda i, j, k: (0, j)))
        args.append(extra)
    elif epi == "residual":
        in_specs.append(pl.BlockSpec((tm, tn), lambda i, j, k: (i, j)))
        args.append(extra)
    return pl.pallas_call(
        functools.partial(_mm_kernel, nk=nk, epi=epi, alpha=alpha),
        grid=(M // tm, N // tn, nk),
        in_specs=in_specs,
        out_specs=pl.BlockSpec((tm, tn), lambda i, j, k: (i, j)),
        out_shape=jax.ShapeDtypeStruct((M, N), out_dtype),
        compiler_params=_cparams(("parallel", "parallel", "arbitrary")),
        name="mm_" + epi,
    )(*args)


def _mm_ln_kernel(a_ref, w_ref, x_ref, g_ref, b_ref, of_ref, ob_ref, *, alpha):
    y = alpha * x_ref[...] + _dot(a_ref[...], w_ref[...])
    mu = jnp.mean(y, axis=-1, keepdims=True)
    yc = y - mu
    var = jnp.mean(yc * yc, axis=-1, keepdims=True)
    y = yc * lax.rsqrt(var + LN_EPS) * g_ref[...] + b_ref[...]
    of_ref[...] = y
    ob_ref[...] = y.astype(BF16)


def matmul_residual_ln(a, w, x, g, b, *, alpha, tm=256):
    M, K = a.shape
    _, N = w.shape
    tm = min(tm, M)
    row = pl.BlockSpec((tm, N), lambda i: (i, 0))
    vec = pl.BlockSpec((1, N), lambda i: (0, 0))
    return pl.pallas_call(
        functools.partial(_mm_ln_kernel, alpha=alpha),
        grid=(M // tm,),
        in_specs=[pl.BlockSpec((tm, K), lambda i: (i, 0)), pl.BlockSpec((K, N), lambda i: (0, 0)), row, vec, vec],
        out_specs=[row, row],
        out_shape=[jax.ShapeDtypeStruct((M, N), F32), jax.ShapeDtypeStruct((M, N), BF16)],
        compiler_params=_cparams(("parallel",)),
        name="mm_residual_ln",
    )(a, w, x, g.reshape(1, N), b.reshape(1, N))


CAST_BLOCK_BYTES = 8 * 1024 * 1024


def _cast_kernel(x_ref, o_ref):
    o_ref[...] = x_ref[...].astype(o_ref.dtype)


def cast_bf16(w, layer=None):
    R, N = w.shape[-2:]
    tr = 8
    while tr * 2 <= min(R, CAST_BLOCK_BYTES // (4 * N)):
        tr *= 2
    assert R % tr == 0
    out_spec = pl.BlockSpec((tr, N), lambda i: (i, 0))
    in_spec = out_spec if layer is None else pl.BlockSpec((None, tr, N), lambda i: (layer, i, 0))
    return pl.pallas_call(
        _cast_kernel, grid=(R // tr,), in_specs=[in_spec], out_specs=out_spec,
        out_shape=jax.ShapeDtypeStruct((R, N), BF16),
        compiler_params=_cparams(("parallel",)), name="cast_bf16",
    )(w)


W_IN_SPLIT_LANES = 256


def _split_w_in_kernel(x_ref, main_ref, small_ref):
    o_glow = 2 * GLA_QK_W + GLA_V_W
    o_gr = o_glow + GLA_RANK
    o_mq = o_gr + GLA_V_W
    o_dq = o_mq + 3 * MOBA_W
    o_da = o_dq + 2 * GDN_K_W + GDN_V_W
    o_dz = o_da + 2 * GDN_HEADS
    pos = 0
    for lo, hi in ((o_mq, o_dq), (0, o_glow), (o_gr, o_mq), (o_dq, o_da), (o_dz, o_dz + GDN_V_W)):
        main_ref[pos:pos + hi - lo, :] = x_ref[lo:hi, :].astype(BF16)
        pos += hi - lo
    small_ref[...] = jnp.zeros_like(small_ref)
    small_ref[SM_GLOW:SM_GLOW + GLA_RANK, :] = x_ref[o_glow:o_gr, :].astype(BF16)
    small_ref[SM_DA:SM_DA + 2 * GDN_HEADS, :] = x_ref[o_da:o_dz, :].astype(BF16)


def split_w_in(w_t, layer):
    _, N, D = w_t.shape
    tc = W_IN_SPLIT_LANES
    return pl.pallas_call(
        _split_w_in_kernel, grid=(D // tc,),
        in_specs=[pl.BlockSpec((None, N, tc), lambda i: (layer, 0, i))],
        out_specs=[pl.BlockSpec((MAIN_W, tc), lambda i: (0, i)), pl.BlockSpec((SMALL_W, tc), lambda i: (0, i))],
        out_shape=[jax.ShapeDtypeStruct((MAIN_W, D), BF16), jax.ShapeDtypeStruct((SMALL_W, D), BF16)],
        compiler_params=_cparams(("parallel",)), name="split_w_in",
    )(w_t)


def _mm_nt_kernel(a_ref, wt_ref, o_ref):
    o_ref[...] = _dot_nt(a_ref[...], wt_ref[...]).astype(o_ref.dtype)


def matmul_nt(a, w_t, *, tm, tn, out_dtype=F32):
    M, K = a.shape
    N, _ = w_t.shape
    tm, tn = min(tm, M), min(tn, N)
    assert M % tm == 0 and N % tn == 0
    return pl.pallas_call(
        _mm_nt_kernel,
        grid=(M // tm, N // tn),
        in_specs=[pl.BlockSpec((tm, K), lambda i, j: (i, 0)), pl.BlockSpec((tn, K), lambda i, j: (j, 0))],
        out_specs=pl.BlockSpec((tm, tn), lambda i, j: (i, j)),
        out_shape=jax.ShapeDtypeStruct((M, N), out_dtype),
        compiler_params=_cparams(("parallel", "parallel")),
        name="mm_nt",
    )(a, w_t)


def _ln_kernel(x_ref, g_ref, b_ref, of_ref, ob_ref):
    x = x_ref[...]
    mu = jnp.mean(x, axis=-1, keepdims=True)
    xc = x - mu
    var = jnp.mean(xc * xc, axis=-1, keepdims=True)
    y = xc * lax.rsqrt(var + LN_EPS) * g_ref[...] + b_ref[...]
    of_ref[...] = y
    ob_ref[...] = y.astype(BF16)


def layer_norm(x, g, b, *, tm=256):
    M, D = x.shape
    tm = min(tm, M)
    row = pl.BlockSpec((tm, D), lambda i: (i, 0))
    vec = pl.BlockSpec((1, D), lambda i: (0, 0))
    return pl.pallas_call(
        _ln_kernel,
        grid=(M // tm,),
        in_specs=[row, vec, vec],
        out_specs=[row, row],
        out_shape=[jax.ShapeDtypeStruct((M, D), F32), jax.ShapeDtypeStruct((M, D), BF16)],
        compiler_params=_cparams(("parallel",)),
        name="layer_norm",
    )(x, g.reshape(1, D), b.reshape(1, D))


def _merge_kernel(ya_ref, yb_ref, yc_ref, w_ref, g0_ref, g1_ref, g2_ref, o_ref):
    a0, a1 = GLA_V_W, GLA_V_W + MOBA_W
    acc = g0_ref[...] * _dot(ya_ref[...], w_ref[0:a0, :])
    acc += g1_ref[...] * _dot(yb_ref[...], w_ref[a0:a1, :])
    acc += g2_ref[...] * _dot(yc_ref[...], w_ref[a1:, :])
    o_ref[...] = acc.astype(o_ref.dtype)


def gated_merge(y_gla, y_moba, y_gdn, w_branch, gates, *, tm=1024, tn=512):
    M = y_gla.shape[0]
    KW, D = w_branch.shape
    tm, tn = min(tm, M), min(tn, D)
    nj = D // tn

    def yspec(width):
        return pl.BlockSpec((tm, width), lambda i, j: (i, 0))

    def gspec(branch):
        return pl.BlockSpec((tm, tn), lambda i, j: (i, branch * nj + j))

    return pl.pallas_call(
        _merge_kernel,
        grid=(M // tm, nj),
        in_specs=[yspec(GLA_V_W), yspec(MOBA_W), yspec(GDN_V_W),
                  pl.BlockSpec((KW, tn), lambda i, j: (0, j)),
                  gspec(0), gspec(1), gspec(2)],
        out_specs=pl.BlockSpec((tm, tn), lambda i, j: (i, j)),
        out_shape=jax.ShapeDtypeStruct((M, D), BF16),
        compiler_params=_cparams(("parallel", "parallel")),
        name="gated_merge",
    )(y_gla, y_moba, y_gdn, w_branch, gates, gates, gates)


def _gla_kernel(q_ref, k_ref, v_ref, r_ref, sm_ref, wg2_ref, bg_ref, nw_ref, o_ref, st_ref, *, nc):
    C = GLA_CHUNK

    @pl.when(pl.program_id(1) == 0)
    def _():
        st_ref[...] = jnp.zeros_like(st_ref)

    glow = sm_ref[:, SM_GLOW:SM_GLOW + GLA_RANK]
    z = _dot(glow, wg2_ref[...], precision=HIGHEST) + bg_ref[...]
    log_a = _log_sigmoid(z) * (1.0 / GLA_TAU)
    ri = lax.broadcasted_iota(jnp.int32, (C, C), 0)
    ci = lax.broadcasted_iota(jnp.int32, (C, C), 1)
    causal = ci <= ri
    tri = jnp.where(causal, 1.0, 0.0).astype(F32)
    nw = nw_ref[...]
    scale = GLA_DK ** -0.5

    for c in range(nc):
        cs = slice(c * C, (c + 1) * C)
        G = _dot(tri, log_a[cs, :], precision=HIGHEST)
        g_mid = G[C // 2 - 1:C // 2, :]
        g_last = G[C - 1:C, :]
        qc = q_ref[cs, :] * scale
        kc = k_ref[cs, :]
        q_in = (qc * jnp.exp(G - g_mid)).astype(BF16)
        k_in = (kc * jnp.exp(g_mid - G)).astype(BF16)
        k_end = (kc * jnp.exp(g_last - G)).astype(BF16)
        q_dec = (qc * jnp.exp(G)).astype(BF16)
        dec = jnp.exp(g_last)
        heads = [(h, slice(h * GLA_DK, (h + 1) * GLA_DK), slice(h * GLA_DV, (h + 1) * GLA_DV))
                 for h in range(GLA_HEADS)]
        A = [_dot_nt(q_in[:, hs], k_in[:, hs]) for _, hs, _ in heads]
        v_b = [v_ref[cs, vs].astype(BF16) for _, _, vs in heads]
        st = [st_ref[h] for h, _, _ in heads]
        o_inter = [_dot_nt(q_dec[:, hs], st[h].astype(BF16)) for h, hs, _ in heads]
        o_intra = [_dot(jnp.where(causal, A[h], 0.0).astype(BF16), v_b[h]) for h, _, _ in heads]
        upd = [_dot_tn(v_b[h], k_end[:, hs]) for h, hs, _ in heads]
        for h, hs, vs in heads:
            st_ref[h] = st[h] * dec[:, hs] + upd[h]
            o = o_intra[h] + o_inter[h]
            r_h = r_ref[cs, vs]
            o = o * lax.rsqrt(jnp.mean(o * o, axis=-1, keepdims=True) + RMS_EPS) * nw
            o_ref[cs, vs] = (o * (r_h * _sigmoid(r_h))).astype(o_ref.dtype)


def gla_mixer(proj, small, w_g2, b_g, norm_w, *, batch, seq, ts=256):
    T = batch * seq
    ts = min(ts, seq)
    nt = seq // ts
    nc = ts // GLA_CHUNK

    def col(width, off):
        return pl.BlockSpec((ts, width), lambda b, t: (b * nt + t, off // width))

    def full(shape):
        return pl.BlockSpec(shape, lambda b, t: (0,) * len(shape))

    return pl.pallas_call(
        functools.partial(_gla_kernel, nc=nc),
        grid=(batch, nt),
        in_specs=[col(GLA_QK_W, OFF_GQ), col(GLA_QK_W, OFF_GK), col(GLA_V_W, OFF_GV), col(GLA_V_W, OFF_GR),
                  pl.BlockSpec((ts, SMALL_W), lambda b, t: (b * nt + t, 0)),
                  full((GLA_RANK, GLA_QK_W)), full((1, GLA_QK_W)), full((1, GLA_DV))],
        out_specs=pl.BlockSpec((ts, GLA_V_W), lambda b, t: (b * nt + t, 0)),
        out_shape=jax.ShapeDtypeStruct((T, GLA_V_W), BF16),
        scratch_shapes=[pltpu.VMEM((GLA_HEADS, GLA_DV, GLA_DK), F32)],
        compiler_params=_cparams(("parallel", "arbitrary")),
        name="gla_mixer",
    )(proj, proj, proj, proj, small, w_g2, b_g.reshape(1, -1), norm_w.reshape(1, -1))


def _rope_tables(seq):
    half = ROT_DIM // 2
    inv = jnp.power(ROPE_THETA, -jnp.arange(half, dtype=F32) * 2.0 / ROT_DIM)
    ang = jnp.arange(seq, dtype=jnp.int32).astype(F32)[:, None] * inv[None, :]
    cos, sin = jnp.cos(ang), jnp.sin(ang)
    pad = jnp.zeros((seq, MOBA_DH - ROT_DIM), F32)
    zero = jnp.zeros((seq, half), F32)
    t_cos = jnp.concatenate([cos, cos, pad + 1.0], axis=-1)
    t_lo = jnp.concatenate([-sin, zero, pad], axis=-1)
    t_hi = jnp.concatenate([zero, sin, pad], axis=-1)
    return t_cos, t_lo, t_hi


def _moba_prep_kernel(q_ref, k_ref, v_ref, tc_ref, tl_ref, th_ref, qo_ref, ko_ref, vt_ref, km_ref):
    half = ROT_DIM // 2
    tc, tl, th = tc_ref[...], tl_ref[...], th_ref[...]

    def rot(x):
        return x * tc + pltpu.roll(x, MOBA_DH - half, axis=1) * tl + pltpu.roll(x, half, axis=1) * th

    for h in range(MOBA_HEADS):
        hs = slice(h * MOBA_DH, (h + 1) * MOBA_DH)
        qo_ref[:, hs] = rot(q_ref[:, hs])
        kr = rot(k_ref[:, hs])
        ko_ref[:, hs] = kr.astype(BF16)
        km_ref[0, :, hs] = jnp.mean(kr, axis=0, keepdims=True)
        vt_ref[0, h, 0:MOBA_DH, :] = v_ref[:, hs].T.astype(BF16)
        vt_ref[0, h, MOBA_DH:, :] = jnp.ones((MOBA_VT_ROWS - MOBA_DH, MOBA_BLOCK), BF16)


def _moba_attn_kernel(q_ref, k_ref, vt_ref, km_ref, o_ref, qs_ref, acc_ref, sel_ref, *, topk, heads):
    i = pl.program_id(2)
    Bk = MOBA_BLOCK
    nb = km_ref.shape[0]
    hslices = [slice(g * MOBA_DH, (g + 1) * MOBA_DH) for g in range(heads)]
    key = lax.broadcasted_iota(jnp.int32, (Bk, Bk), 0)
    qry = lax.broadcasted_iota(jnp.int32, (Bk, Bk), 1)
    blk = lax.broadcasted_iota(jnp.int32, (nb, Bk), 0)
    own = pl.ds(pl.multiple_of(i * Bk, Bk), Bk)

    for g, hs in enumerate(hslices):
        qs_ref[g] = (q_ref[:, hs] * (MOBA_DH ** -0.5 * LOG2_E)).astype(BF16)
    scores = [_dot_nt(k_ref[own, hs], qs_ref[g]) for g, hs in enumerate(hslices)]
    gates = [_dot_nt(km_ref[:, hs], q_ref[:, hs], precision=HIGHEST) for hs in hslices]
    for g in range(heads):
        gate = jnp.where(blk < i, gates[g], -jnp.inf)
        sel = jnp.zeros(gate.shape, F32)
        for r in range(topk):
            top = jnp.max(gate, axis=0, keepdims=True)
            idx = jnp.min(jnp.where(gate == top, blk, nb), axis=0, keepdims=True)
            hit = blk == idx
            sel = jnp.where(jnp.logical_and(hit, i > r), 1.0, sel)
            gate = jnp.where(hit, -jnp.inf, gate)
        sel_ref[g] = sel
    init, probs = [], []
    for g in range(heads):
        s = jnp.where(key <= qry, scores[g], -jnp.inf)
        m0 = jnp.max(s, axis=0, keepdims=True)
        init.append(m0)
        probs.append(jnp.exp2(s - m0).astype(BF16))
    for g in range(heads):
        acc_ref[g] = _dot(vt_ref[i, g], probs[g])

    def visit(blocks, carry):
        sc = [[_dot_nt(k_ref[pl.ds(pl.multiple_of(j * Bk, Bk), Bk), hs], qs_ref[g]) for g, hs in enumerate(hslices)]
              for j in blocks]
        m_run = list(carry)
        pr, scale = [], []
        for n, j in enumerate(blocks):
            pr.append([])
            scale.append([])
            for g in range(heads):
                picked = sel_ref[g, pl.ds(j, 1), :] > 0.5
                m_blk = jnp.where(picked, jnp.max(sc[n][g], axis=0, keepdims=True), -jnp.inf)
                m_new = jnp.maximum(m_run[g], m_blk)
                pr[n].append(jnp.exp2(sc[n][g] - jnp.where(picked, m_new, jnp.inf)).astype(BF16))
                scale[n].append(jnp.exp2(m_run[g] - m_new))
                m_run[g] = m_new
        pv = [[_dot(vt_ref[j, g], pr[n][g]) for g in range(heads)] for n, j in enumerate(blocks)]
        for g in range(heads):
            acc = acc_ref[g]
            for n in range(len(blocks)):
                acc = scale[n][g] * acc + pv[n][g]
            acc_ref[g] = acc
        return tuple(m_run)

    carry, done = tuple(init), 0
    for width in MOBA_BLOCKS_PER_TRIP:
        trips = (i - done) // width
        carry = lax.fori_loop(0, trips, lambda t, c, w=width, d=done: visit([d + t * w + n for n in range(w)], c), carry)
        done = done + trips * width
    for g, hs in enumerate(hslices):
        acc = acc_ref[g]
        o_ref[:, hs] = (acc[0:MOBA_DH, :] / acc[MOBA_DH:MOBA_DH + 1, :]).T.astype(o_ref.dtype)


def moba_mixer(proj, *, batch, seq):
    T = batch * seq
    Bk = MOBA_BLOCK
    assert seq % Bk == 0
    nb = seq // Bk
    topk = max(1, min(MOBA_TOPK, nb - 1))
    t_cos, t_lo, t_hi = _rope_tables(seq)

    def col(off):
        return pl.BlockSpec((Bk, MOBA_W), lambda t: (t, off // MOBA_W))

    tab = pl.BlockSpec((Bk, MOBA_DH), lambda t: (t % nb, 0))
    wide = pl.BlockSpec((Bk, MOBA_W), lambda t: (t, 0))
    q_rot, k_rot, v_t, k_mean = pl.pallas_call(
        _moba_prep_kernel,
        grid=(T // Bk,),
        in_specs=[col(OFF_MQ), col(OFF_MK), col(OFF_MV), tab, tab, tab],
        out_specs=[wide, wide, pl.BlockSpec((1, MOBA_HEADS, MOBA_VT_ROWS, Bk), lambda t: (t, 0, 0, 0)),
                   pl.BlockSpec((1, 1, MOBA_W), lambda t: (t, 0, 0))],
        out_shape=[jax.ShapeDtypeStruct((T, MOBA_W), F32), jax.ShapeDtypeStruct((T, MOBA_W), BF16),
                   jax.ShapeDtypeStruct((T // Bk, MOBA_HEADS, MOBA_VT_ROWS, Bk), BF16),
                   jax.ShapeDtypeStruct((T // Bk, 1, MOBA_W), F32)],
        compiler_params=_cparams(("parallel",)),
        name="moba_prep",
    )(proj, proj, proj, t_cos, t_lo, t_hi)
    k_mean = k_mean.reshape(batch, nb, MOBA_W)
    v_t = v_t.reshape(batch, nb, MOBA_HEADS, MOBA_VT_ROWS, Bk)

    G = MOBA_HEADS_PER_STEP
    GW = G * MOBA_DH
    qo = pl.BlockSpec((Bk, GW), lambda b, h, i: (b * nb + i, h))
    return pl.pallas_call(
        functools.partial(_moba_attn_kernel, topk=topk, heads=G),
        grid=(batch, MOBA_HEADS // G, nb),
        in_specs=[qo, pl.BlockSpec((seq, GW), lambda b, h, i: (b, h)),
                  pl.BlockSpec((None, nb, G, MOBA_VT_ROWS, Bk), lambda b, h, i: (b, 0, h, 0, 0)),
                  pl.BlockSpec((None, nb, GW), lambda b, h, i: (b, 0, h))],
        out_specs=qo,
        out_shape=jax.ShapeDtypeStruct((T, MOBA_W), BF16),
        scratch_shapes=[pltpu.VMEM((G, Bk, MOBA_DH), BF16), pltpu.VMEM((G, MOBA_VT_ROWS, Bk), F32),
                        pltpu.VMEM((G, nb, Bk), F32)],
        compiler_params=_cparams(("parallel", "parallel", "arbitrary")),
        name="moba_attn",
    )(q_rot, k_rot, v_t, k_mean)


def _gdn_kernel(q_ref, k_ref, v_ref, z_ref, sm_ref, wq_ref, wk_ref, wv_ref, al_ref, dt_ref, nw_ref,
                o_ref, tail_ref, xbuf_ref, st_ref, *, nc, ts):
    C = GDN_CHUNK
    PADR = 8
    h = pl.program_id(1)

    @pl.when(pl.program_id(2) == 0)
    def _():
        tail_ref[...] = jnp.zeros_like(tail_ref)
        st_ref[...] = jnp.zeros_like(st_ref)

    def conv_silu(n, x_ref, w_ref):
        xbuf_ref[n, 0:PADR, :] = tail_ref[n]
        xbuf_ref[n, PADR:PADR + ts, :] = x_ref[...]
        acc = None
        for j in range(GDN_CONV):
            lo = PADR - (GDN_CONV - 1) + j
            term = w_ref[j:j + 1, :] * xbuf_ref[n, lo:lo + ts, :]
            acc = term if acc is None else acc + term
        tail_ref[n] = x_ref[ts - PADR:ts, :]
        return acc * _sigmoid(acc)

    q = conv_silu(0, q_ref, wq_ref)
    k = conv_silu(1, k_ref, wk_ref)
    v = conv_silu(2, v_ref, wv_ref)
    q = q * lax.rsqrt(jnp.sum(q * q, axis=-1, keepdims=True) + RMS_EPS) * (GDN_DK ** -0.5)
    k = k * lax.rsqrt(jnp.sum(k * k, axis=-1, keepdims=True) + RMS_EPS)

    sm = sm_ref[...]
    lane = lax.broadcasted_iota(jnp.int32, sm.shape, 1)
    g_all = -jnp.exp(al_ref[...]) * _softplus(sm + dt_ref[...])
    g = jnp.sum(jnp.where(lane == SM_DA + h, g_all, 0.0), axis=1, keepdims=True)
    beta = jnp.sum(jnp.where(lane == SM_DB + h, _sigmoid(sm), 0.0), axis=1, keepdims=True)

    ri = lax.broadcasted_iota(jnp.int32, (C, C), 0)
    ci = lax.broadcasted_iota(jnp.int32, (C, C), 1)
    incl = ci <= ri
    strict = ci < ri
    eye = ci == ri
    nw = nw_ref[...]

    chunks = [slice(c * C, (c + 1) * C) for c in range(nc)]
    kb = [k[cs, :].astype(BF16) for cs in chunks]
    kk = [_dot_nt(x, x) for x in kb]
    qk_raw = [_dot_nt(q[cs, :].astype(BF16), x) for cs, x in zip(chunks, kb)]
    X, Pb, qk, q_dec, k_end, g_end = [], [], [], [], [], []
    for c, cs in enumerate(chunks):
        g_c, beta_c = g[cs, :], beta[cs, :]
        gam_row = jnp.sum(jnp.where(ri <= ci, g_c, 0.0), axis=0, keepdims=True)
        gam_col = jnp.sum(jnp.where(eye, gam_row, 0.0), axis=1, keepdims=True)
        gam_last = gam_col[C - 1:C, :]
        decay = jnp.where(incl, jnp.exp(jnp.where(incl, gam_col - gam_row, 0.0)), 0.0)
        e_gam = jnp.exp(gam_col)
        Pb.append(jnp.where(strict, beta_c * kk[c] * decay, 0.0).astype(BF16))
        X.append(jnp.concatenate([v[cs, :] * beta_c, k[cs, :] * (beta_c * e_gam)], axis=-1))
        qk.append(jnp.where(incl, qk_raw[c] * decay, 0.0).astype(BF16))
        q_dec.append(q[cs, :] * e_gam)
        k_end.append((k[cs, :] * jnp.exp(gam_last - gam_col)).astype(BF16))
        g_end.append(jnp.exp(gam_last))

    def apply(P, x, sign):
        hi, lo = _split_bf16(x)
        return x + sign * (_dot(P, hi) + _dot(P, lo))

    X = [apply(P, x, -1.0) for P, x in zip(Pb, X)]
    for _ in range(5):
        Pb = [_dot(P, P).astype(BF16) for P in Pb]
        X = [apply(P, x, 1.0) for P, x in zip(Pb, X)]

    ub = [x[:, :GDN_DV].astype(BF16) for x in X]
    wb = [x[:, GDN_DV:].astype(BF16) for x in X]
    s_op = [(-_dot_tn(k_end[c], wb[c])).astype(BF16) for c in range(nc)]
    s_add = [_dot_tn(k_end[c], ub[c]) for c in range(nc)]
    o_op = [(q_dec[c] - _dot(qk[c], wb[c])).astype(BF16) for c in range(nc)]
    o_add = [_dot(qk[c], ub[c]) for c in range(nc)]
    outs = []
    for c in range(nc):
        st = st_ref[...]
        sb = st.astype(BF16)
        outs.append(_dot(o_op[c], sb) + o_add[c])
        st_ref[...] = g_end[c] * st + (_dot(s_op[c], sb) + s_add[c])
    for cs, o in zip(chunks, outs):
        z_c = z_ref[cs, :]
        o = o * lax.rsqrt(jnp.mean(o * o, axis=-1, keepdims=True) + RMS_EPS) * nw
        o_ref[cs, :] = (o * (z_c * _sigmoid(z_c))).astype(o_ref.dtype)


def gdn_mixer(proj, small, conv_w, a_log, dt_bias, norm_w, *, batch, seq, ts=1024):
    T = batch * seq
    ts = min(ts, seq)
    nt = seq // ts
    nc = ts // GDN_CHUNK
    H = GDN_HEADS

    def col(off):
        return pl.BlockSpec((ts, LANE), lambda b, h, t: (b * nt + t, off // LANE + h))

    def wcol(off):
        return pl.BlockSpec((GDN_CONV, LANE), lambda b, h, t: (0, off // LANE + h))

    def vec():
        return pl.BlockSpec((1, LANE), lambda b, h, t: (0, 0))

    a_row = jnp.zeros((1, SMALL_W), F32).at[0, SM_DA:SM_DA + H].set(a_log.astype(F32))
    dt_row = jnp.zeros((1, SMALL_W), F32).at[0, SM_DA:SM_DA + H].set(dt_bias.astype(F32))
    return pl.pallas_call(
        functools.partial(_gdn_kernel, nc=nc, ts=ts),
        grid=(batch, H, nt),
        in_specs=[col(OFF_DQ), col(OFF_DK), col(OFF_DV), col(OFF_DZ),
                  pl.BlockSpec((ts, SMALL_W), lambda b, h, t: (b * nt + t, 0)),
                  wcol(0), wcol(GDN_K_W), wcol(2 * GDN_K_W), vec(), vec(), vec()],
        out_specs=pl.BlockSpec((ts, LANE), lambda b, h, t: (b * nt + t, h)),
        out_shape=jax.ShapeDtypeStruct((T, GDN_V_W), BF16),
        scratch_shapes=[pltpu.VMEM((3, 8, LANE), F32), pltpu.VMEM((3, ts + 8, LANE), F32),
                        pltpu.VMEM((GDN_DK, GDN_DV), F32)],
        compiler_params=_cparams(("parallel", "parallel", "arbitrary")),
        name="gdn_mixer",
    )(proj, proj, proj, proj, small, conv_w, conv_w, conv_w, a_row, dt_row, norm_w.reshape(1, -1))


def _cross_kernel(q_ref, k_ref, v_ref, o_ref):
    scale = CROSS_DH ** -0.5
    for h in range(CROSS_HEADS):
        hs = slice(h * CROSS_DH, (h + 1) * CROSS_DH)
        s = _dot_nt(q_ref[:, hs], k_ref[:, hs]) * scale
        p = jnp.exp(s - jnp.max(s, axis=1, keepdims=True))
        o = _dot(p.astype(BF16), v_ref[:, hs]) / jnp.sum(p, axis=1, keepdims=True)
        o_ref[:, hs] = o.astype(o_ref.dtype)


def cross_attention(q, k, v, *, batch, seq, mem_len, tq=512):
    T, W = q.shape
    tq = min(tq, seq)
    nt = seq // tq
    kv = pl.BlockSpec((mem_len, W), lambda b, t: (b, 0))
    qo = pl.BlockSpec((tq, W), lambda b, t: (b * nt + t, 0))
    return pl.pallas_call(
        _cross_kernel,
        grid=(batch, nt),
        in_specs=[qo, kv, kv],
        out_specs=qo,
        out_shape=jax.ShapeDtypeStruct((T, W), BF16),
        compiler_params=_cparams(("parallel", "parallel")),
        name="cross_attn",
    )(q, k, v)


def kernel(x, mem, w_in, gla_w_g2, gla_b_g, gla_norm_w, gdn_conv_w, gdn_a_log, gdn_dt_bias, gdn_norm_w, w_branch, w_gate, b_gate, w_out, ln1_g, ln1_b, w_cq, w_ck, w_cv, w_co, ln2_g, ln2_b, w_up, w_down, ln3_g, ln3_b):
    B, S, D = x.shape
    M = mem.shape[1]
    depth = w_in.shape[0]
    alpha = (2.0 * depth) ** 0.25
    T = B * S
    h = x.reshape(T, D).astype(F32)
    hb = cast_bf16(h)
    mem_b = cast_bf16(mem.reshape(B * M, D))
    big = dict(tm=1024, tn=1024)
    w_in_t = jnp.swapaxes(w_in, 1, 2)
    for l in range(depth):
        w_main_t, w_small_t = split_w_in(w_in_t, l)
        proj = matmul_nt(hb, w_main_t, **big)
        small = matmul_nt(hb, w_small_t, **big)
        y_gla = gla_mixer(proj, small, gla_w_g2[l], gla_b_g[l], gla_norm_w[l], batch=B, seq=S)
        y_moba = moba_mixer(proj, batch=B, seq=S)
        y_gdn = gdn_mixer(proj, small, gdn_conv_w[l], gdn_a_log[l], gdn_dt_bias[l], gdn_norm_w[l],
                          batch=B, seq=S)
        gates = matmul(hb, cast_bf16(w_gate, l), epi="bias_sigmoid", extra=b_gate[l].reshape(1, -1), **big)
        merged = gated_merge(y_gla, y_moba, y_gdn, cast_bf16(w_branch, l), gates)
        pre = matmul(merged, cast_bf16(w_out, l), epi="residual", extra=h, alpha=alpha, **big)
        h, hb = layer_norm(pre, ln1_g[l], ln1_b[l])

        cq = matmul(hb, cast_bf16(w_cq, l), out_dtype=BF16, **big)
        ck = matmul(mem_b, cast_bf16(w_ck, l), out_dtype=BF16, **big)
        cv = matmul(mem_b, cast_bf16(w_cv, l), out_dtype=BF16, **big)
        att = cross_attention(cq, ck, cv, batch=B, seq=S, mem_len=M)
        h, hb = matmul_residual_ln(att, cast_bf16(w_co, l), h, ln2_g[l], ln2_b[l], alpha=alpha)

        a = matmul(hb, cast_bf16(w_up, l), epi="relu2", out_dtype=BF16, **big)
        pre = matmul(a, cast_bf16(w_down, l), epi="residual", extra=h, alpha=alpha, tk=2048, **big)
        h, hb = layer_norm(pre, ln3_g[l], ln3_b[l])
    return h.reshape(B, S, D).astype(x.dtype)
```
